```python
import jax, jax.numpy as jnp
from jax import lax
import numpy as np

D_MODEL = 1024
BATCH = 8
SEQ = 2048
DEPTH = 4

CHUNK = 64
N_MEM = 256
MIX_WIDTH = D_MODEL
N_GROUPS = 4
GROUP_WIDTH = MIX_WIDTH // N_GROUPS
HEAD_DIM = 64
N_HEADS_A = GROUP_WIDTH // HEAD_DIM
N_HEADS_C = GROUP_WIDTH // HEAD_DIM
N_HEADS_D = GROUP_WIDTH // HEAD_DIM
CONV_WIDTH_A = 4
POOL_WINDOWS = (2, 4, 8, 16)
POOL_GROUP = GROUP_WIDTH // len(POOL_WINDOWS)
LORA_DECAY = 64
LORA_ICLR = 64
Q_BLOCK = 128
N_MEM_HEADS = 4
MEM_HEAD_DIM = D_MODEL // N_MEM_HEADS
NORM_EPS = 1e-6
GN_EPS = 64e-5
MLSTM_F_BIAS = (3.0, 6.0)
FOX_F_BIAS = (1.0, 4.0)

IN_LAYOUT = (
    ("a_q", GROUP_WIDTH), ("a_k", GROUP_WIDTH), ("a_v", GROUP_WIDTH),
    ("a_i", N_HEADS_A), ("a_f", N_HEADS_A), ("a_z", GROUP_WIDTH),
    ("b_x", GROUP_WIDTH), ("b_z", GROUP_WIDTH),
    ("c_r", GROUP_WIDTH), ("c_k", GROUP_WIDTH), ("c_v", GROUP_WIDTH),
    ("c_w", LORA_DECAY), ("c_a", LORA_ICLR), ("c_z", GROUP_WIDTH),
    ("d_q", GROUP_WIDTH), ("d_k", GROUP_WIDTH), ("d_v", GROUP_WIDTH),
    ("d_f", N_HEADS_D), ("d_z", GROUP_WIDTH),
)
IN_OFFSETS = {name: sum(s for _, s in IN_LAYOUT[:i]) for i, (name, _) in enumerate(IN_LAYOUT)}
IN_SPLITS = tuple(IN_OFFSETS[name] for name, _ in IN_LAYOUT[1:])
N_IN = sum(s for _, s in IN_LAYOUT)
MU_WIDTH = 3 * GROUP_WIDTH + LORA_DECAY + LORA_ICLR
MU_SPLITS = (GROUP_WIDTH, 2 * GROUP_WIDTH, 3 * GROUP_WIDTH, 3 * GROUP_WIDTH + LORA_DECAY)

kernel_name = "hybrid_parallel_group_streaming_encoder"


def rms_norm(x, g):
    x32 = x.astype(jnp.float32)
    y = x32 * lax.rsqrt(jnp.mean(x32 * x32, axis=-1, keepdims=True) + NORM_EPS)
    return (y * g).astype(x.dtype)


def split_heads(u, n_heads):
    b, s, _ = u.shape
    return u.reshape(b, s, n_heads, -1).transpose(0, 2, 1, 3)


def merge_heads(u):
    b, h, s, d = u.shape
    return u.transpose(0, 2, 1, 3).reshape(b, s, h * d)


def causal_depthwise_conv(u, w):
    c, k = u.shape[-1], w.shape[0]
    return lax.conv_general_dilated(
        u, w[:, None, :].astype(u.dtype), window_strides=(1,), padding=[(k - 1, 0)],
        dimension_numbers=("NWC", "WIO", "NWC"), feature_group_count=c)


def token_shift(u, mu):
    prev = jnp.pad(u, ((0, 0), (1, 0), (0, 0)))[:, :-1]
    return u + mu * (prev - u)


def head_rms_norm(y, g):
    y32 = y.astype(jnp.float32)
    y32 = y32 * lax.rsqrt(jnp.mean(y32 * y32, axis=-1, keepdims=True) + NORM_EPS)
    return merge_heads(y32) * g


def mlstm_chunkwise(q, k, v, i_pre, f_pre):
    f32 = jnp.float32
    b_, h_, s_, dh = q.shape
    nc, L = s_ // CHUNK, CHUNK
    qc = q.astype(f32).reshape(b_, h_, nc, L, dh)
    kc = (k.astype(f32) * dh ** -0.5).reshape(b_, h_, nc, L, dh)
    vc = v.astype(f32).reshape(b_, h_, nc, L, dh)
    ig = i_pre.astype(f32).reshape(b_, h_, nc, L)
    bcum = jnp.cumsum(jax.nn.log_sigmoid(f_pre.astype(f32)).reshape(b_, h_, nc, L), axis=-1)
    g = bcum[..., -1]
    w_loc = g[..., None] - bcum + ig
    m_loc = jnp.max(w_loc, axis=-1)
    e_loc = jnp.exp(w_loc - m_loc[..., None])
    c_loc = jnp.einsum("bhcs,bhcsd,bhcse->bhcde", e_loc, vc, kc)
    n_loc = jnp.einsum("bhcs,bhcse->bhce", e_loc, kc)

    def step(carry, xs):
        c_prev, n_prev, m_prev = carry
        g_c, m_l, c_l, n_l = xs
        m_new = jnp.maximum(g_c + m_prev, m_l)
        a = jnp.exp(g_c + m_prev - m_new)
        bb = jnp.exp(m_l - m_new)
        c_new = a[..., None, None] * c_prev + bb[..., None, None] * c_l
        n_new = a[..., None] * n_prev + bb[..., None] * n_l
        return (c_new, n_new, m_new), (c_prev, n_prev, m_prev)

    init = (jnp.zeros((b_, h_, dh, dh), f32), jnp.zeros((b_, h_, dh), f32), jnp.zeros((b_, h_), f32))
    xs = (jnp.moveaxis(g, 2, 0), jnp.moveaxis(m_loc, 2, 0), jnp.moveaxis(c_loc, 2, 0), jnp.moveaxis(n_loc, 2, 0))
    _, (c_in, n_in, m_in) = lax.scan(step, init, xs)
    c_in = jnp.moveaxis(c_in, 0, 2)
    n_in = jnp.moveaxis(n_in, 0, 2)
    m_in = jnp.moveaxis(m_in, 0, 2)
    causal = jnp.tril(jnp.ones((L, L), dtype=bool))
    dmat = jnp.where(causal, bcum[..., :, None] - bcum[..., None, :] + ig[..., None, :], -jnp.inf)
    m_inter = bcum + m_in[..., None]
    m_t = jnp.maximum(m_inter, jnp.max(dmat, axis=-1))
    s_qk = jnp.einsum("bhctd,bhcsd->bhcts", qc, kc) * jnp.exp(dmat - m_t[..., None])
    inter_w = jnp.exp(m_inter - m_t)
    num = jnp.einsum("bhcts,bhcsd->bhctd", s_qk, vc) + inter_w[..., None] * jnp.einsum("bhcde,bhcte->bhctd", c_in, qc)
    den = jnp.sum(s_qk, axis=-1) + inter_w * jnp.einsum("bhce,bhcte->bhct", n_in, qc)
    h = num / jnp.maximum(jnp.abs(den), jnp.exp(-m_t))[..., None]
    return h.reshape(b_, h_, s_, dh)


def multiscale_pool(u, w_pool, scale):
    u32 = u.astype(jnp.float32)
    b_, s_, _ = u.shape
    cs = jnp.cumsum(u32, axis=1)
    pos = jnp.arange(s_, dtype=jnp.float32)
    outs = []
    for gi, win in enumerate(POOL_WINDOWS):
        sl = slice(gi * POOL_GROUP, (gi + 1) * POOL_GROUP)
        cg = cs[..., sl]
        shifted = jnp.pad(cg, ((0, 0), (win, 0), (0, 0)))[:, :s_]
        count = jnp.minimum(pos + 1.0, float(win))
        outs.append((cg - shifted) / count[None, :, None] - u32[..., sl])
    pooled = jnp.stack(outs, axis=2)
    mixed = jnp.einsum("bsgc,gcd->bsgd", pooled, w_pool).reshape(b_, s_, GROUP_WIDTH)
    return mixed * scale


def rwkv7_time_mix(r, k, v, w_lo, a_lo, mu, w0, w2, a0, a2, key_k, key_a, bonus_u, gn_g, gn_b):
    f32 = jnp.float32
    mu_r, mu_k, mu_v, mu_w, mu_a = jnp.split(mu.astype(f32), MU_SPLITS)
    r = token_shift(r.astype(f32), mu_r)
    k = token_shift(k.astype(f32), mu_k)
    v = token_shift(v.astype(f32), mu_v)
    w_lo = token_shift(w_lo.astype(f32), mu_w)
    a_lo = token_shift(a_lo.astype(f32), mu_a)
    w_log = -jax.nn.softplus(-(w0 + jnp.tanh(w_lo) @ w2.astype(f32))) - 0.5
    decay = jnp.exp(-jnp.exp(w_log))
    a = jax.nn.sigmoid(a0 + a_lo @ a2.astype(f32))
    b_, s_, _ = r.shape
    hsplit = lambda u: u.reshape(b_, s_, N_HEADS_C, HEAD_DIM)
    kk = hsplit(k * key_k)
    kk = kk / jnp.maximum(jnp.sqrt(jnp.sum(kk * kk, axis=-1, keepdims=True)), 1e-12)
    k = k * (1.0 + (a - 1.0) * key_a)
    rh, kh, vh, wh, ah = hsplit(r), hsplit(k), hsplit(v), hsplit(decay), hsplit(a)

    def step(state, xs):
        r_t, k_t, v_t, w_t, kk_t, a_t = xs
        sa = jnp.einsum("bhvk,bhk->bhv", state, -kk_t)
        state = state * w_t[:, :, None, :] + sa[..., None] * (kk_t * a_t)[:, :, None, :] + v_t[..., None] * k_t[:, :, None, :]
        return state, jnp.einsum("bhvk,bhk->bhv", state, r_t)

    xs = tuple(u.transpose(1, 0, 2, 3) for u in (rh, kh, vh, wh, kk, ah))
    init = jnp.zeros((b_, N_HEADS_C, HEAD_DIM, HEAD_DIM), f32)
    _, y = lax.scan(step, init, xs)
    y = y.transpose(1, 0, 2, 3)
    mean = jnp.mean(y, axis=-1, keepdims=True)
    var = jnp.mean(jnp.square(y - mean), axis=-1, keepdims=True)
    y = ((y - mean) * lax.rsqrt(var + GN_EPS)).reshape(b_, s_, GROUP_WIDTH) * gn_g + gn_b
    bonus = jnp.sum(rh * kh * bonus_u.astype(f32).reshape(N_HEADS_C, HEAD_DIM), axis=-1, keepdims=True) * vh
    return y + bonus.reshape(b_, s_, GROUP_WIDTH)


def forgetting_attention(q, k, v, f_pre, qk_g):
    dh, s_len = q.shape[-1], q.shape[2]
    q = rms_norm(q, qk_g[0]) * (dh ** -0.5)
    k = rms_norm(k, qk_g[1])
    log_f_cum = jnp.cumsum(jax.nn.log_sigmoid(f_pre.astype(jnp.float32)), axis=-1)
    outs = []
    for blk in range(s_len // Q_BLOCK):
        q0, q1 = blk * Q_BLOCK, (blk + 1) * Q_BLOCK
        logits = jnp.einsum("bhtd,bhsd->bhts", q[:, :, q0:q1], k[:, :, :q1]).astype(jnp.float32)
        logits = logits + log_f_cum[:, :, q0:q1, None] - log_f_cum[:, :, None, :q1]
        causal = (q0 + jnp.arange(Q_BLOCK))[:, None] >= jnp.arange(q1)[None, :]
        p = jax.nn.softmax(jnp.where(causal, logits, -jnp.inf), axis=-1)
        outs.append(jnp.einsum("bhts,bhsd->bhtd", p.astype(v.dtype), v[:, :, :q1]))
    return jnp.concatenate(outs, axis=2)


def memory_cross_attention(h, mem_n, wq, wkv, wo):
    q = split_heads(h @ wq, N_MEM_HEADS)
    k, v = jnp.split(mem_n @ wkv, 2, axis=-1)
    k, v = split_heads(k, N_MEM_HEADS), split_heads(v, N_MEM_HEADS)
    logits = jnp.einsum("bhtd,bhmd->bhtm", q, k).astype(jnp.float32) * (MEM_HEAD_DIM ** -0.5)
    p = jax.nn.softmax(logits, axis=-1)
    o = jnp.einsum("bhtm,bhmd->bhtd", p.astype(v.dtype), v)
    return merge_heads(o) @ wo


def setup_inputs(seed: int = 0) -> dict:
    key = jax.random.key(seed)
    ks = jax.random.split(key, 32)
    f32 = jnp.float32
    nrm = lambda kk, shape, scale: scale * jax.random.normal(kk, shape, f32)
    L, D, GW = DEPTH, D_MODEL, GROUP_WIDTH
    b_in = nrm(ks[5], (L, N_IN), 0.02)
    fa, fd = IN_OFFSETS["a_f"], IN_OFFSETS["d_f"]
    b_in = b_in.at[:, fa:fa + N_HEADS_A].add(jnp.linspace(MLSTM_F_BIAS[0], MLSTM_F_BIAS[1], N_HEADS_A, dtype=f32))
    b_in = b_in.at[:, fd:fd + N_HEADS_D].add(jnp.linspace(FOX_F_BIAS[0], FOX_F_BIAS[1], N_HEADS_D, dtype=f32))
    return {
        "x": nrm(ks[0], (BATCH, SEQ, D), 1.0),
        "mem": nrm(ks[1], (BATCH, N_MEM, D), 1.0),
        "pre_norm_g": 1.0 + nrm(ks[2], (L, D), 0.05),
        "post_norm_g": 1.0 + nrm(ks[3], (L, D), 0.05),
        "w_in": nrm(ks[4], (L, D, N_IN), D ** -0.5),
        "b_in": b_in,
        "conv_a": nrm(ks[6], (L, CONV_WIDTH_A, 2 * GW), CONV_WIDTH_A ** -0.5),
        "norm_a_g": 1.0 + nrm(ks[7], (L, GW), 0.05),
        "pool_w": nrm(ks[8], (L, len(POOL_WINDOWS), POOL_GROUP, POOL_GROUP), POOL_GROUP ** -0.5),
        "pool_scale": 0.5 + nrm(ks[9], (L, GW), 0.05),
        "shift_mu_c": jax.random.uniform(ks[10], (L, MU_WIDTH), f32),
        "decay_w0": jax.random.uniform(ks[11], (L, GW), f32, -3.0, 1.0),
        "decay_w2": nrm(ks[12], (L, LORA_DECAY, GW), 0.1),
        "iclr_a0": nrm(ks[13], (L, GW), 0.1),
        "iclr_a2": nrm(ks[14], (L, LORA_ICLR, GW), 0.1),
        "key_k": 0.85 + nrm(ks[15], (L, GW), 0.05),
        "key_a": 1.0 + nrm(ks[16], (L, GW), 0.05),
        "bonus_u": nrm(ks[17], (L, GW), 0.1),
        "gn_c_g": 1.0 + nrm(ks[18], (L, GW), 0.05),
        "gn_c_b": nrm(ks[19], (L, GW), 0.02),
        "qk_norm_d": 1.0 + nrm(ks[20], (L, 2, HEAD_DIM), 0.05),
        "w_out": nrm(ks[21], (L, MIX_WIDTH, D), MIX_WIDTH ** -0.5),
        "mem_norm_g": 1.0 + nrm(ks[22], (D,), 0.05),
        "xattn_pre_g": 1.0 + nrm(ks[23], (L, D), 0.05),
        "xattn_post_g": 1.0 + nrm(ks[24], (L, D), 0.05),
        "xattn_wq": nrm(ks[25], (L, D, D), D ** -0.5),
        "xattn_wkv": nrm(ks[26], (L, D, 2 * D), D ** -0.5),
        "xattn_wo": nrm(ks[27], (L, D, D), D ** -0.5),
    }


def reference(x, mem, pre_norm_g, post_norm_g, w_in, b_in, conv_a, norm_a_g, pool_w, pool_scale,
              shift_mu_c, decay_w0, decay_w2, iclr_a0, iclr_a2, key_k, key_a, bonus_u, gn_c_g, gn_c_b,
              qk_norm_d, w_out, mem_norm_g, xattn_pre_g, xattn_post_g, xattn_wq, xattn_wkv, xattn_wo):
    mem_n = rms_norm(mem, mem_norm_g)
    for l in range(DEPTH):
        h = rms_norm(x, pre_norm_g[l])
        proj = h @ w_in[l] + b_in[l]
        (a_q, a_k, a_v, a_i, a_f, a_z, b_x, b_z, c_r, c_k, c_v, c_w, c_a, c_z,
         d_q, d_k, d_v, d_f, d_z) = jnp.split(proj, IN_SPLITS, axis=-1)
        a_qk = jax.nn.silu(causal_depthwise_conv(jnp.concatenate([a_q, a_k], axis=-1), conv_a[l]))
        a_q, a_k = jnp.split(a_qk, 2, axis=-1)
        h_a = mlstm_chunkwise(split_heads(a_q, N_HEADS_A), split_heads(a_k, N_HEADS_A),
                              split_heads(a_v, N_HEADS_A), a_i.transpose(0, 2, 1), a_f.transpose(0, 2, 1))
        y_a = head_rms_norm(h_a, norm_a_g[l]) * jax.nn.silu(a_z)
        y_b = multiscale_pool(b_x, pool_w[l], pool_scale[l]) * jax.nn.silu(b_z)
        y_c = rwkv7_time_mix(c_r, c_k, c_v, c_w, c_a, shift_mu_c[l], decay_w0[l], decay_w2[l],
                             iclr_a0[l], iclr_a2[l], key_k[l], key_a[l], bonus_u[l],
                             gn_c_g[l], gn_c_b[l]) * jax.nn.silu(c_z)
        h_d = forgetting_attention(split_heads(d_q, N_HEADS_D), split_heads(d_k, N_HEADS_D),
                                   split_heads(d_v, N_HEADS_D), d_f.transpose(0, 2, 1), qk_norm_d[l])
        y_d = merge_heads(h_d) * jax.nn.silu(d_z)
        y = jnp.concatenate([u.astype(x.dtype) for u in (y_a, y_b, y_c, y_d)], axis=-1) @ w_out[l]
        x = x + rms_norm(y, post_norm_g[l])
        hm = rms_norm(x, xattn_pre_g[l])
        x = x + rms_norm(memory_cross_attention(hm, mem_n, xattn_wq[l], xattn_wkv[l], xattn_wo[l]),
                         xattn_post_g[l])
    return x
```

```python
import functools

import jax
import jax.numpy as jnp
from jax import lax
from jax.experimental import pallas as pl
from jax.experimental.pallas import tpu as pltpu

F32 = jnp.float32
BF16 = jnp.bfloat16

HEAD_DIM = 64
N_HEADS = 4
GROUP_WIDTH = HEAD_DIM * N_HEADS
LANES = 128
NORM_EPS = 1e-6
GN_EPS = 64e-5
POOL_WINDOWS = (2, 4, 8, 16)
CONV_WIDTH_A = 4
LORA = 64
MLSTM_CHUNK = 128
RWKV_CHUNK = 64
RWKV_SUB = 16
FOX_QBLOCK = 256
ROW_TILE = 512
VMEM_LIMIT = 56 * 1024 * 1024

SEG_A = 4 * GROUP_WIDTH
SEG_B = 2 * GROUP_WIDTH
SEG_C = 4 * GROUP_WIDTH + 2 * LORA
SEG_D = 4 * GROUP_WIDTH
SEG_G = LANES
N_PROJ = SEG_A + SEG_B + SEG_C + SEG_D + SEG_G


def _lane(shape):
    return lax.broadcasted_iota(jnp.int32, shape, len(shape) - 1)


def _row(shape):
    return lax.broadcasted_iota(jnp.int32, shape, len(shape) - 2)


def _nt(a, b):
    return lax.dot_general(a, b, (((1,), (1,)), ((), ())), preferred_element_type=F32)


def _tn(a, b):
    return lax.dot_general(a, b, (((0,), (0,)), ((), ())), preferred_element_type=F32)


def _nn(a, b):
    return jnp.dot(a, b, preferred_element_type=F32)


def _sigmoid(x):
    return 1.0 / (1.0 + jnp.exp(-x))


def _silu(x):
    return x * _sigmoid(x)


def _softplus(x):
    return jnp.maximum(x, 0.0) + jnp.log(1.0 + jnp.exp(-jnp.abs(x)))


def _log_sigmoid(x):
    return -_softplus(-x)


def _shift_rows(x, s):
    rolled = pltpu.roll(x, shift=s, axis=0)
    return jnp.where(_row(x.shape) >= s, rolled, 0.0)


def _cumsum_rows(x, seg):
    pos = _row(x.shape) & (seg - 1)
    s = 1
    while s < seg:
        x = x + jnp.where(pos >= s, pltpu.roll(x, shift=s, axis=0), 0.0)
        s *= 2
    return x


def _stack_heads(x):
    hid = _lane(x.shape) >> 6
    zero = jnp.zeros_like(x)
    return jnp.concatenate([jnp.where(hid == h, x, zero) for h in range(N_HEADS)], axis=0)


def _per_head(cols, shape):
    hid = _lane(shape) >> 6
    out = jnp.broadcast_to(cols[N_HEADS - 1], shape)
    for h in range(N_HEADS - 2, -1, -1):
        out = jnp.where(hid == h, cols[h], out)
    return out


def _per_head_rows(vals, shape):
    hid = _row(shape) >> 6
    out = jnp.broadcast_to(vals[N_HEADS - 1], shape)
    for h in range(N_HEADS - 2, -1, -1):
        out = jnp.where(hid == h, vals[h], out)
    return out


def _block_ones(n):
    return jnp.where((_row((n, n)) >> 6) == (_lane((n, n)) >> 6), 1.0, 0.0).astype(BF16)


def _group_sum(x, ones_bd):
    hi = x.astype(BF16)
    r1 = x - hi.astype(F32)
    mid = r1.astype(BF16)
    lo = (r1 - mid.astype(F32)).astype(BF16)
    return _nn(hi, ones_bd) + _nn(mid, ones_bd) + _nn(lo, ones_bd)


def _rms_rows(x, g):
    return x * lax.rsqrt(jnp.mean(x * x, axis=-1, keepdims=True) + NORM_EPS) * g


def _params(**kw):
    return pltpu.CompilerParams(vmem_limit_bytes=VMEM_LIMIT, **kw)


def _in_proj_kernel(x_ref, g_ref, w_ref, b_ref, oa_ref, ob_ref, oc_ref, od_ref, og_ref):
    h = _rms_rows(x_ref[...], g_ref[...]).astype(BF16)
    off = 0
    for o_ref, width in ((oa_ref, SEG_A), (ob_ref, SEG_B), (oc_ref, SEG_C), (od_ref, SEG_D), (og_ref, SEG_G)):
        acc = _nn(h, w_ref[:, off:off + width]) + b_ref[:, off:off + width]
        o_ref[...] = acc.astype(o_ref.dtype)
        off += width


def _in_proj(x2, g, w, b):
    m, d = x2.shape
    tm = min(ROW_TILE, m)
    widths = (SEG_A, SEG_B, SEG_C, SEG_D, SEG_G)
    dtypes = (BF16, BF16, BF16, BF16, F32)
    return pl.pallas_call(
        _in_proj_kernel,
        grid=(m // tm,),
        in_specs=[
            pl.BlockSpec((tm, d), lambda i: (i, 0)),
            pl.BlockSpec((1, d), lambda i: (0, 0)),
            pl.BlockSpec((d, N_PROJ), lambda i: (0, 0)),
            pl.BlockSpec((1, N_PROJ), lambda i: (0, 0)),
        ],
        out_specs=[pl.BlockSpec((tm, wd), lambda i: (i, 0)) for wd in widths],
        out_shape=[jax.ShapeDtypeStruct((m, wd), dt) for wd, dt in zip(widths, dtypes)],
        compiler_params=_params(),
        name="in_proj",
    )(x2, g, w, b)


def _mlstm_kernel(a_ref, g_ref, conv_ref, ng_ref, o_ref,
                  q_s, k_s, bc_s, gt_s, bt_s, c_s, n_s, m_s):
    s_len = a_ref.shape[0]
    L = MLSTM_CHUNK
    nc = s_len // L
    gw = GROUP_WIDTH

    qk = a_ref[:, :2 * gw].astype(F32)
    acc = qk * conv_ref[CONV_WIDTH_A - 1:CONV_WIDTH_A, :]
    for j in range(1, CONV_WIDTH_A):
        acc = acc + _shift_rows(qk, j) * conv_ref[CONV_WIDTH_A - 1 - j:CONV_WIDTH_A - j, :]
    acc = _silu(acc)
    q_s[...] = acc[:, :gw].astype(BF16)
    k_s[...] = (acc[:, gw:] * (HEAD_DIM ** -0.5)).astype(BF16)

    gates = g_ref[...]
    bcum = _cumsum_rows(_log_sigmoid(gates), L)
    bc_s[...] = bcum
    for c in range(nc):
        gt_s[c] = jnp.transpose(gates[c * L:(c + 1) * L, :])
        bt_s[c] = jnp.transpose(bcum[c * L:(c + 1) * L, :])

    c_s[...] = jnp.zeros_like(c_s)
    n_s[...] = jnp.zeros_like(n_s)
    m_s[...] = jnp.zeros_like(m_s)
    ones_bd = _block_ones(gw)
    tril = _lane((L, L)) <= _row((L, L))
    bd_mask = (_row((gw, gw)) >> 6) == (_lane((gw, gw)) >> 6)

    def chunk(c, carry):
        r0 = pl.multiple_of(c * L, L)
        q = q_s[pl.ds(r0, L), :]
        k = k_s[pl.ds(r0, L), :]
        v = a_ref[pl.ds(r0, L), 2 * gw:3 * gw]
        z = a_ref[pl.ds(r0, L), 3 * gw:4 * gw].astype(F32)
        g_col = g_ref[pl.ds(r0, L), :]
        b_col = bc_s[pl.ds(r0, L), :]
        g_row = gt_s[c]
        b_row = bt_s[c]
        b_last = bc_s[pl.ds(r0 + L - 1, 1), :]
        m_prev = m_s[...]
        hid = _lane((L, gw)) >> 6

        s_list, mt_cols, iw_cols, dint_cols, e_cols = [], [], [], [], []
        a_sc, b_sc, m_new = [], [], []
        for h in range(N_HEADS):
            bt = b_col[:, 4 + h:5 + h]
            bs = b_row[4 + h:5 + h, :]
            ig_s = g_row[h:h + 1, :]
            ig_c = g_col[:, h:h + 1]
            g_tot = b_last[:, 4 + h:5 + h]
            m_in = m_prev[:, h:h + 1]
            qh = jnp.where(hid == h, q, jnp.zeros_like(q))
            logits = _nt(qh, k)
            dmat = jnp.where(tril, bt - bs + ig_s, -jnp.inf)
            m_inter = bt + m_in
            m_t = jnp.maximum(m_inter, jnp.max(dmat, axis=-1, keepdims=True))
            s_qk = logits * jnp.exp(dmat - m_t)
            s_list.append(s_qk.astype(BF16))
            dint_cols.append(jnp.sum(s_qk, axis=-1, keepdims=True))
            mt_cols.append(m_t)
            iw_cols.append(jnp.exp(m_inter - m_t))
            w_loc = g_tot - bt + ig_c
            m_loc = jnp.max(w_loc, axis=0, keepdims=True)
            e_cols.append(jnp.exp(w_loc - m_loc))
            mn = jnp.maximum(g_tot + m_in, m_loc)
            m_new.append(mn)
            a_sc.append(jnp.exp(g_tot + m_in - mn))
            b_sc.append(jnp.exp(m_loc - mn))

        shape = (L, gw)
        c_prev = c_s[...]
        n_prev = n_s[...]
        num_intra = _nn(jnp.concatenate(s_list, axis=1), _stack_heads(v))
        num_inter = _nt(q, c_prev.astype(BF16))
        den_inter = _group_sum(q.astype(F32) * n_prev, ones_bd)
        iw = _per_head(iw_cols, shape)
        num = num_intra + iw * num_inter
        den = _per_head(dint_cols, shape) + iw * den_inter
        hh = num / jnp.maximum(jnp.abs(den), jnp.exp(-_per_head(mt_cols, shape)))
        ms = _group_sum(hh * hh, ones_bd) * (1.0 / HEAD_DIM)
        y = hh * lax.rsqrt(ms + NORM_EPS) * ng_ref[...] * _silu(z)
        o_ref[pl.ds(r0, L), :] = y.astype(o_ref.dtype)

        e_full = _per_head(e_cols, shape)
        ve = (v.astype(F32) * e_full).astype(BF16)
        c_loc = jnp.where(bd_mask, _tn(ve, k), 0.0)
        n_loc = jnp.sum(e_full * k.astype(F32), axis=0, keepdims=True)
        c_s[...] = _per_head_rows(a_sc, (gw, gw)) * c_prev + _per_head_rows(b_sc, (gw, gw)) * c_loc
        n_s[...] = _per_head(a_sc, (1, gw)) * n_prev + _per_head(b_sc, (1, gw)) * n_loc
        lane = _lane(m_prev.shape)
        m_next = m_prev
        for h in range(N_HEADS):
            m_next = jnp.where(lane == h, m_new[h], m_next)
        m_s[...] = m_next
        return carry

    lax.fori_loop(0, nc, chunk, 0)


def _mlstm(a, gates, conv, ng):
    b, s, _ = a.shape
    gw = GROUP_WIDTH
    nc = s // MLSTM_CHUNK
    return pl.pallas_call(
        _mlstm_kernel,
        grid=(b,),
        in_specs=[
            pl.BlockSpec((None, s, SEG_A), lambda i: (i, 0, 0)),
            pl.BlockSpec((None, s, SEG_G), lambda i: (i, 0, 0)),
            pl.BlockSpec((CONV_WIDTH_A, 2 * gw), lambda i: (0, 0)),
            pl.BlockSpec((1, gw), lambda i: (0, 0)),
        ],
        out_specs=pl.BlockSpec((None, s, gw), lambda i: (i, 0, 0)),
        out_shape=jax.ShapeDtypeStruct((b, s, gw), BF16),
        scratch_shapes=[
            pltpu.VMEM((s, gw), BF16), pltpu.VMEM((s, gw), BF16),
            pltpu.VMEM((s, SEG_G), F32),
            pltpu.VMEM((nc, SEG_G, MLSTM_CHUNK), F32), pltpu.VMEM((nc, SEG_G, MLSTM_CHUNK), F32),
            pltpu.VMEM((gw, gw), F32), pltpu.VMEM((1, gw), F32), pltpu.VMEM((1, LANES), F32),
        ],
        compiler_params=_params(),
        name="mlstm",
    )(a, gates, conv, ng)


def _pool_kernel(b_ref, w_ref, sc_ref, o_ref):
    gw = GROUP_WIDTH
    u = b_ref[:, :gw].astype(F32)
    z = b_ref[:, gw:].astype(F32)
    pos = _row(u.shape).astype(F32)
    grp = _lane(u.shape) >> 6
    pooled = jnp.zeros_like(u)
    run, width = u, 1
    for gi, win in enumerate(POOL_WINDOWS):
        while width < win:
            run = run + _shift_rows(run, width)
            width *= 2
        count = jnp.minimum(pos + 1.0, float(win))
        pooled = jnp.where(grp == gi, run / count - u, pooled)
    mixed = _nn(pooled.astype(BF16), w_ref[...])
    o_ref[...] = (mixed * sc_ref[...] * _silu(z)).astype(o_ref.dtype)


def _pool(bx, w_bd, scale):
    b, s, _ = bx.shape
    gw = GROUP_WIDTH
    return pl.pallas_call(
        _pool_kernel,
        grid=(b,),
        in_specs=[
            pl.BlockSpec((None, s, SEG_B), lambda i: (i, 0, 0)),
            pl.BlockSpec((gw, gw), lambda i: (0, 0)),
            pl.BlockSpec((1, gw), lambda i: (0, 0)),
        ],
        out_specs=pl.BlockSpec((None, s, gw), lambda i: (i, 0, 0)),
        out_shape=jax.ShapeDtypeStruct((b, s, gw), BF16),
        compiler_params=_params(),
        name="pool",
    )(bx, w_bd, scale)


(_P_MU_R, _P_MU_K, _P_MU_V, _P_W0, _P_A0, _P_KEY_K, _P_KEY_A, _P_BONUS, _P_GN_G, _P_GN_B, _P_MU_WA) = range(11)
_P_ROWS = 16


def _cat_mm(x, y):
    return _nn(x.astype(BF16), _stack_heads(y.astype(BF16)))


def _unit_lower_inverse(a):
    t = RWKV_CHUNK
    shape = a.shape
    rb = _row(shape) >> 4
    col = _lane(shape) & (t - 1)
    cb = col >> 4
    eye = jnp.where(col == _row(shape), 1.0, 0.0)
    d = jnp.where(rb == cb, a, 0.0)
    e = jnp.where(rb == cb, 0.0, a)
    d2 = _cat_mm(d, d)
    d4 = _cat_mm(d2, d2)
    d8 = _cat_mm(d4, d4)
    x = eye + d
    x = x + _cat_mm(x, d2)
    x = x + _cat_mm(x, d4)
    dinv = x + _cat_mm(x, d8)
    f = _cat_mm(dinv, e)
    f2 = _cat_mm(f, f)
    w = eye + f
    w = w + _cat_mm(w, f2)
    return _cat_mm(w, dinv)


def _rwkv_kernel(c_ref, p_ref, w2a2_ref, o_ref,
                 pt_s, rt_s, qt_s, kt_s, qh_s, kh_s, v_s, ge_s, y_s, st_s):
    s_len = c_ref.shape[0]
    gw = GROUP_WIDTH
    T = RWKV_CHUNK
    nc = s_len // T
    prm = lambda i: p_ref[i:i + 1, :]

    def shifted(u, mu):
        return u + mu * (_shift_rows(u, 1) - u)

    r = shifted(c_ref[:, 0:gw].astype(F32), prm(_P_MU_R))
    k = shifted(c_ref[:, gw:2 * gw].astype(F32), prm(_P_MU_K))
    v = shifted(c_ref[:, 2 * gw:3 * gw].astype(F32), prm(_P_MU_V))
    wa = shifted(c_ref[:, 4 * gw:4 * gw + 2 * LORA].astype(F32), p_ref[_P_MU_WA:_P_MU_WA + 1, :2 * LORA])
    wa = jnp.where(_lane(wa.shape) < LORA, jnp.tanh(wa), wa)
    lora = _nn(wa.astype(BF16), w2a2_ref[...])
    w_log = -_softplus(-(prm(_P_W0) + lora[:, :gw])) - 0.5
    log_decay = -jnp.exp(w_log)
    a = _sigmoid(prm(_P_A0) + lora[:, gw:])

    ones_bd = _block_ones(gw)
    kk = k * prm(_P_KEY_K)
    kk = kk / jnp.maximum(jnp.sqrt(_group_sum(kk * kk, ones_bd)), 1e-12)
    k = k * (1.0 + (a - 1.0) * prm(_P_KEY_A))
    bonus = _group_sum(r * k * prm(_P_BONUS), ones_bd) * v

    lc = _cumsum_rows(log_decay, T)
    lc_end = lc
    s = 1
    while s < T:
        pos = _row(lc.shape) & (T - 1)
        lc_end = jnp.where(pos < T - s, pltpu.roll(lc_end, shift=s_len - s, axis=0), lc_end)
        s *= 2
    g_inv = jnp.exp(-lc)
    g_end = jnp.exp(lc_end - lc)
    pt_s[...] = (-kk * jnp.exp(lc - log_decay)).astype(BF16)
    rt_s[...] = (r * jnp.exp(lc)).astype(BF16)
    qt_s[...] = (kk * a * g_inv).astype(BF16)
    kt_s[...] = (k * g_inv).astype(BF16)
    qh_s[...] = (kk * a * g_end).astype(BF16)
    kh_s[...] = (k * g_end).astype(BF16)
    v_s[...] = v.astype(BF16)
    ge_s[...] = jnp.exp(lc_end)
    st_s[...] = jnp.zeros_like(st_s)

    col = _lane((T, N_HEADS * T)) & (T - 1)
    row = _row((T, N_HEADS * T))
    strict = col < row
    incl = col <= row
    bd_mask = (_row((gw, gw)) >> 6) == (_lane((gw, gw)) >> 6)

    def chunk(c, carry):
        r0 = pl.multiple_of(c * T, T)
        sl = pl.ds(r0, T)
        pr = jnp.concatenate([pt_s[sl, :], rt_s[sl, :]], axis=0)
        qk_st = jnp.concatenate([_stack_heads(qt_s[sl, :]), _stack_heads(kt_s[sl, :])], axis=0)
        g = _nt(pr, qk_st)
        a_pq = jnp.where(strict, g[:T, :N_HEADS * T], 0.0)
        a_pk = jnp.where(strict, g[:T, N_HEADS * T:], 0.0)
        a_rq = jnp.where(incl, g[T:, :N_HEADS * T], 0.0)
        a_rk = jnp.where(incl, g[T:, N_HEADS * T:], 0.0)
        tinv = _unit_lower_inverse(a_pq)
        st = st_s[...]
        prs = _nt(pr, st.astype(BF16))
        vb = v_s[sl, :]
        v_st = _stack_heads(vb)
        b = prs[:T] + _nn(a_pk.astype(BF16), v_st)
        u = _nn(tinv.astype(BF16), _stack_heads(b.astype(BF16)))
        ub = u.astype(BF16)
        y = prs[T:] + _nn(jnp.concatenate([a_rq, a_rk], axis=1).astype(BF16),
                          jnp.concatenate([_stack_heads(ub), v_st], axis=0))
        y_s[sl, :] = y
        ds = _tn(jnp.concatenate([ub, vb], axis=0),
                 jnp.concatenate([qh_s[sl, :], kh_s[sl, :]], axis=0))
        st_s[...] = st * ge_s[pl.ds(r0 + T - 1, 1), :] + jnp.where(bd_mask, ds, 0.0)
        return carry

    lax.fori_loop(0, nc, chunk, 0)

    y = y_s[...]
    mean = _group_sum(y, ones_bd) * (1.0 / HEAD_DIM)
    yc = y - mean
    var = _group_sum(yc * yc, ones_bd) * (1.0 / HEAD_DIM)
    y = yc * lax.rsqrt(var + GN_EPS) * prm(_P_GN_G) + prm(_P_GN_B) + bonus
    z = c_ref[:, 3 * gw:4 * gw].astype(F32)
    o_ref[...] = (y * _silu(z)).astype(o_ref.dtype)


def _rwkv(cx, prm, w2a2):
    b, s, _ = cx.shape
    gw = GROUP_WIDTH
    return pl.pallas_call(
        _rwkv_kernel,
        grid=(b,),
        in_specs=[
            pl.BlockSpec((None, s, SEG_C), lambda i: (i, 0, 0)),
            pl.BlockSpec((_P_ROWS, gw), lambda i: (0, 0)),
            pl.BlockSpec((2 * LORA, 2 * gw), lambda i: (0, 0)),
        ],
        out_specs=pl.BlockSpec((None, s, gw), lambda i: (i, 0, 0)),
        out_shape=jax.ShapeDtypeStruct((b, s, gw), BF16),
        scratch_shapes=[pltpu.VMEM((s, gw), BF16) for _ in range(7)] + [
            pltpu.VMEM((s, gw), F32), pltpu.VMEM((s, gw), F32), pltpu.VMEM((gw, gw), F32)],
        compiler_params=_params(),
        name="rwkv",
    )(cx, prm, w2a2)


def _fox_kernel(d_ref, g_ref, qkg_ref, o_ref, q_s, k_s, ft_s):
    s_len = d_ref.shape[0]
    gw = GROUP_WIDTH
    tq = min(FOX_QBLOCK, s_len)
    ones_bd = _block_ones(gw)

    def head_norm(u, g):
        ms = _group_sum(u * u, ones_bd) * (1.0 / HEAD_DIM)
        return (u * lax.rsqrt(ms + NORM_EPS) * g).astype(BF16)

    q_s[...] = (head_norm(d_ref[:, 0:gw].astype(F32), qkg_ref[0:1, :]).astype(F32) * (HEAD_DIM ** -0.5)).astype(BF16)
    k_s[...] = head_norm(d_ref[:, gw:2 * gw].astype(F32), qkg_ref[1:2, :])
    fcum = _cumsum_rows(_log_sigmoid(g_ref[...]), s_len)
    ft_s[...] = jnp.transpose(fcum)

    for blk in range(s_len // tq):
        q0, q1 = blk * tq, (blk + 1) * tq
        q = q_s[q0:q1, :]
        k = k_s[0:q1, :]
        v = d_ref[0:q1, 2 * gw:3 * gw]
        causal = _lane((tq, q1)) <= _row((tq, q1)) + q0
        hid = _lane((tq, gw)) >> 6
        out = jnp.zeros((tq, gw), F32)
        for h in range(N_HEADS):
            qh = jnp.where(hid == h, q, jnp.zeros_like(q))
            logits = _nt(qh, k) - ft_s[8 + h:9 + h, 0:q1]
            logits = jnp.where(causal, logits, -jnp.inf)
            mx = jnp.max(logits, axis=-1, keepdims=True)
            p = jnp.exp(logits - mx)
            inv = 1.0 / jnp.sum(p, axis=-1, keepdims=True)
            vh = jnp.where((_lane(v.shape) >> 6) == h, v, jnp.zeros_like(v))
            out = out + _nn(p.astype(BF16), vh) * inv
        z = d_ref[q0:q1, 3 * gw:4 * gw].astype(F32)
        o_ref[q0:q1, :] = (out * _silu(z)).astype(o_ref.dtype)


def _fox(dx, gates, qkg):
    b, s, _ = dx.shape
    gw = GROUP_WIDTH
    return pl.pallas_call(
        _fox_kernel,
        grid=(b,),
        in_specs=[
            pl.BlockSpec((None, s, SEG_D), lambda i: (i, 0, 0)),
            pl.BlockSpec((None, s, SEG_G), lambda i: (i, 0, 0)),
            pl.BlockSpec((2, gw), lambda i: (0, 0)),
        ],
        out_specs=pl.BlockSpec((None, s, gw), lambda i: (i, 0, 0)),
        out_shape=jax.ShapeDtypeStruct((b, s, gw), BF16),
        scratch_shapes=[pltpu.VMEM((s, gw), BF16), pltpu.VMEM((s, gw), BF16), pltpu.VMEM((SEG_G, s), F32)],
        compiler_params=_params(),
        name="fox",
    )(dx, gates, qkg)


def _mem_kv_kernel(mem_ref, g_ref, w_ref, o_ref):
    mn = _rms_rows(mem_ref[...], g_ref[...]).astype(BF16)
    o_ref[...] = _nn(mn, w_ref[...]).astype(o_ref.dtype)


def _mem_kv(mem, g, wkv):
    b, n_mem, d = mem.shape
    depth = wkv.shape[0]
    return pl.pallas_call(
        _mem_kv_kernel,
        grid=(depth, b),
        in_specs=[
            pl.BlockSpec((None, n_mem, d), lambda l, i: (i, 0, 0)),
            pl.BlockSpec((1, d), lambda l, i: (0, 0)),
            pl.BlockSpec((None, d, 2 * d), lambda l, i: (l, 0, 0)),
        ],
        out_specs=pl.BlockSpec((None, None, n_mem, 2 * d), lambda l, i: (l, i, 0, 0)),
        out_shape=jax.ShapeDtypeStruct((depth, b, n_mem, 2 * d), BF16),
        compiler_params=_params(),
        name="mem_kv",
    )(mem, g, wkv)


def _out_xattn_kernel(x_ref, ya_ref, yb_ref, yc_ref, yd_ref, wout_ref, pg_ref, xg_ref, wq_ref, kv_ref, wo_ref,
                      og_ref, o_ref, *, n_mem_heads):
    d = x_ref.shape[-1]
    dh = d // n_mem_heads
    y = jnp.concatenate([ya_ref[...], yb_ref[...], yc_ref[...], yd_ref[...]], axis=-1)
    x = x_ref[...] + _rms_rows(_nn(y, wout_ref[...]), pg_ref[...])
    hm = _rms_rows(x, xg_ref[...]).astype(BF16)
    q = _nn(hm, wq_ref[...]).astype(BF16)
    outs = []
    for h in range(n_mem_heads):
        kh = kv_ref[:, h * dh:(h + 1) * dh]
        vh = kv_ref[:, d + h * dh:d + (h + 1) * dh]
        logits = _nt(q[:, h * dh:(h + 1) * dh], kh) * (dh ** -0.5)
        mx = jnp.max(logits, axis=-1, keepdims=True)
        p = jnp.exp(logits - mx)
        p = p / jnp.sum(p, axis=-1, keepdims=True)
        outs.append(_nn(p.astype(BF16), vh).astype(BF16))
    o = _nn(jnp.concatenate(outs, axis=-1), wo_ref[...])
    o_ref[...] = x + _rms_rows(o, og_ref[...])


def _out_xattn(x2, ys, wout, pg, xg, wq, kv, wo, og, seq_len, n_mem_heads=4):
    m, d = x2.shape
    gw = GROUP_WIDTH
    tm = min(ROW_TILE, seq_len)
    per_seq = seq_len // tm
    n_mem = kv.shape[1]
    row = lambda i: (i, 0)
    fixed = lambda i: (0, 0)
    return pl.pallas_call(
        functools.partial(_out_xattn_kernel, n_mem_heads=n_mem_heads),
        grid=(m // tm,),
        in_specs=[pl.BlockSpec((tm, d), row)] + [pl.BlockSpec((tm, gw), row)] * 4 + [
            pl.BlockSpec((4 * gw, d), fixed),
            pl.BlockSpec((1, d), fixed),
            pl.BlockSpec((1, d), fixed),
            pl.BlockSpec((d, d), fixed),
            pl.BlockSpec((None, n_mem, 2 * d), lambda i: (i // per_seq, 0, 0)),
            pl.BlockSpec((d, d), fixed),
            pl.BlockSpec((1, d), fixed),
        ],
        out_specs=pl.BlockSpec((tm, d), row),
        out_shape=jax.ShapeDtypeStruct((m, d), F32),
        compiler_params=_params(),
        name="out_xattn",
    )(x2, *ys, wout, pg, xg, wq, kv, wo, og)


def _split_in(w):
    gw = GROUP_WIDTH
    sizes = (("a_q", gw), ("a_k", gw), ("a_v", gw), ("a_i", N_HEADS), ("a_f", N_HEADS), ("a_z", gw),
             ("b_x", gw), ("b_z", gw),
             ("c_r", gw), ("c_k", gw), ("c_v", gw), ("c_w", LORA), ("c_a", LORA), ("c_z", gw),
             ("d_q", gw), ("d_k", gw), ("d_v", gw), ("d_f", N_HEADS), ("d_z", gw))
    out, off = {}, 0
    for name, size in sizes:
        out[name] = w[..., off:off + size]
        off += size
    return out


def _relayout_in(w):
    p = _split_in(w)
    pad = jnp.zeros(w.shape[:-1] + (SEG_G - 3 * N_HEADS,), w.dtype)
    order = ("a_q", "a_k", "a_v", "a_z", "b_x", "b_z", "c_r", "c_k", "c_v", "c_z", "c_w", "c_a",
             "d_q", "d_k", "d_v", "d_z", "a_i", "a_f", "d_f")
    return jnp.concatenate([p[n] for n in order] + [pad], axis=-1)


def _block_diag(blocks):
    n, k, _ = blocks.shape
    eye = jnp.eye(n, dtype=blocks.dtype)
    return jnp.einsum("gcd,gh->gchd", blocks, eye).reshape(n * k, n * k)


def kernel(x, mem, pre_norm_g, post_norm_g, w_in, b_in, conv_a, norm_a_g, pool_w, pool_scale, shift_mu_c,
           decay_w0, decay_w2, iclr_a0, iclr_a2, key_k, key_a, bonus_u, gn_c_g, gn_c_b, qk_norm_d, w_out,
           mem_norm_g, xattn_pre_g, xattn_post_g, xattn_wq, xattn_wkv, xattn_wo):
    bsz, seq, d = x.shape
    depth = w_in.shape[0]
    gw = GROUP_WIDTH

    w_in_r = _relayout_in(w_in).astype(BF16)
    b_in_r = _relayout_in(b_in)
    mu = shift_mu_c
    zeros_row = jnp.zeros((depth, gw), F32)
    mu_wa = jnp.concatenate([mu[:, 3 * gw:], jnp.zeros((depth, gw - 2 * LORA), F32)], axis=-1)
    rows = [mu[:, 0:gw], mu[:, gw:2 * gw], mu[:, 2 * gw:3 * gw], decay_w0, iclr_a0, key_k, key_a, bonus_u,
            gn_c_g, gn_c_b, mu_wa] + [zeros_row] * (_P_ROWS - 11)
    rwkv_prm = jnp.stack(rows, axis=1)
    zl = jnp.zeros((depth, LORA, gw), F32)
    w2a2 = jnp.concatenate([jnp.concatenate([decay_w2, zl], axis=-1),
                            jnp.concatenate([zl, iclr_a2], axis=-1)], axis=1).astype(BF16)
    pool_bd = jax.vmap(_block_diag)(pool_w).astype(BF16)
    qkg = jnp.tile(qk_norm_d, (1, 1, N_HEADS))
    w_out_b = w_out.astype(BF16)
    wq_b = xattn_wq.astype(BF16)
    wo_b = xattn_wo.astype(BF16)

    kv_all = _mem_kv(mem, mem_norm_g.reshape(1, d), xattn_wkv.astype(BF16))

    x2 = x.reshape(bsz * seq, d)
    for l in range(depth):
        pa, pb, pc, pd, pg = _in_proj(x2, pre_norm_g[l].reshape(1, d), w_in_r[l], b_in_r[l].reshape(1, -1))
        sh = lambda u: u.reshape(bsz, seq, u.shape[-1])
        gates = sh(pg)
        y_a = _mlstm(sh(pa), gates, conv_a[l], norm_a_g[l].reshape(1, gw))
        y_b = _pool(sh(pb), pool_bd[l], pool_scale[l].reshape(1, gw))
        y_c = _rwkv(sh(pc), rwkv_prm[l], w2a2[l])
        y_d = _fox(sh(pd), gates, qkg[l])
        ys = [u.reshape(bsz * seq, gw) for u in (y_a, y_b, y_c, y_d)]
        x2 = _out_xattn(x2, ys, w_out_b[l], post_norm_g[l].reshape(1, d), xattn_pre_g[l].reshape(1, d),
                        wq_b[l], kv_all[l], wo_b[l], xattn_post_g[l].reshape(1, d), seq)
    return x2.reshape(bsz, seq, d)
```

```python
import functools

import jax
import jax.numpy as jnp
from jax import lax
from jax.experimental import pallas as pl
from jax.experimental.pallas import tpu as pltpu

F32 = jnp.float32
BF16 = jnp.bfloat16

HEAD_DIM = 64
N_HEADS = 4
GROUP_WIDTH = HEAD_DIM * N_HEADS
LANES = 128
SUBLANES = 8
BF16_ROWS = 16
NORM_EPS = 1e-6
GN_EPS = 64e-5
POOL_WINDOWS = (2, 4, 8, 16)
CONV_WIDTH_A = 4
LORA = 64
MLSTM_CHUNK = 128
MLSTM_GROUP = 2
RWKV_CHUNK = 64
RWKV_SUB = 16
RWKV_GROUP = 8
RWKV_OUT_GROUP = 2
RWKV_PAIR = 2
DECAY_SCALE = 0.6065306597126334
FOX_QBLOCK = 256
ROW_TILE = 512
VMEM_LIMIT = 56 * 1024 * 1024

SEG_A = 4 * GROUP_WIDTH
SEG_B = 2 * GROUP_WIDTH
SEG_C = 4 * GROUP_WIDTH + 2 * LORA
SEG_D = 4 * GROUP_WIDTH
SEG_G = LANES
N_PROJ = SEG_A + SEG_B + SEG_C + SEG_D + SEG_G
GATE_I, GATE_F, GATE_D = 0, N_HEADS, 2 * N_HEADS

(_G_PRE, _G_POST, _G_XPRE, _G_XPOST) = range(4)
_G_ROWS = 8

(_P_MU_R, _P_MU_K, _P_MU_V, _P_W0, _P_A0, _P_KEY_K, _P_KEY_A, _P_BONUS, _P_GN_G, _P_GN_B, _P_MU_WA) = range(11)
_P_ROWS = 16


def _lane(shape):
    return lax.broadcasted_iota(jnp.int32, shape, len(shape) - 1)


def _row(shape):
    return lax.broadcasted_iota(jnp.int32, shape, len(shape) - 2)


def _nt(a, b):
    return lax.dot_general(a, b, (((1,), (1,)), ((), ())), preferred_element_type=F32)


def _tn(a, b):
    return lax.dot_general(a, b, (((0,), (0,)), ((), ())), preferred_element_type=F32)


def _nn(a, b):
    return jnp.dot(a, b, preferred_element_type=F32)


def _sigmoid(x):
    return 1.0 / (1.0 + jnp.exp(-x))


def _silu(x):
    return x * _sigmoid(x)


def _softplus(x):
    return jnp.maximum(x, 0.0) + jnp.log(1.0 + jnp.exp(-jnp.abs(x)))


def _log_sigmoid(x):
    return -_softplus(-x)


def _shift_rows(x, s):
    rolled = pltpu.roll(x, shift=s, axis=0)
    return jnp.where(_row(x.shape) >= s, rolled, 0.0)


def _cumsum_rows(x, seg):
    pos = _row(x.shape) & (seg - 1)
    s = 1
    while s < seg:
        x = x + jnp.where(pos >= s, pltpu.roll(x, shift=s, axis=0), 0.0)
        s *= 2
    return x


def _cummax_rows(x, seg):
    pos = _row(x.shape) & (seg - 1)
    s = 1
    while s < seg:
        x = jnp.maximum(x, jnp.where(pos >= s, pltpu.roll(x, shift=s, axis=0), -jnp.inf))
        s *= 2
    return x


def _stack_heads(x):
    hid = _lane(x.shape) >> 6
    zero = jnp.zeros_like(x)
    return jnp.concatenate([jnp.where(hid == h, x, zero) for h in range(N_HEADS)], axis=0)


def _per_head(cols, shape):
    hid = _lane(shape) >> 6
    out = jnp.broadcast_to(cols[N_HEADS - 1], shape)
    for h in range(N_HEADS - 2, -1, -1):
        out = jnp.where(hid == h, cols[h], out)
    return out


def _per_head_rows(vals, shape):
    hid = _row(shape) >> 6
    out = jnp.broadcast_to(vals[N_HEADS - 1], shape)
    for h in range(N_HEADS - 2, -1, -1):
        out = jnp.where(hid == h, vals[h], out)
    return out


def _head_cols(x, first):
    return [x[:, first + h:first + h + 1] for h in range(N_HEADS)]


def _block_ones(n):
    return jnp.where((_row((n, n)) >> 6) == (_lane((n, n)) >> 6), 1.0, 0.0).astype(BF16)


def _group_sum(x, ones_bd):
    hi = x.astype(BF16)
    r1 = x - hi.astype(F32)
    mid = r1.astype(BF16)
    lo = (r1 - mid.astype(F32)).astype(BF16)
    return _nn(hi, ones_bd) + _nn(mid, ones_bd) + _nn(lo, ones_bd)


def _rms_rows(x, g):
    return x * lax.rsqrt(jnp.mean(x * x, axis=-1, keepdims=True) + NORM_EPS) * g


def _params(**kw):
    return pltpu.CompilerParams(vmem_limit_bytes=VMEM_LIMIT, **kw)


def _layer_spec(shape):
    zeros = (0,) * len(shape)
    return pl.BlockSpec((None,) + tuple(shape), lambda i, layer: (layer[0],) + zeros)


def _row_spec(shape):
    zeros = (0,) * (len(shape) - 1)
    return pl.BlockSpec(tuple(shape), lambda i, layer: (i,) + zeros)


def _layered_call(body, grid, in_specs, out_specs, out_shape, scratch_shapes=(), name=None):
    def with_layer(layer_ref, *refs):
        del layer_ref
        body(*refs)

    return pl.pallas_call(
        with_layer,
        grid_spec=pltpu.PrefetchScalarGridSpec(num_scalar_prefetch=1, grid=grid, in_specs=in_specs,
                                               out_specs=out_specs, scratch_shapes=scratch_shapes),
        out_shape=out_shape, compiler_params=_params(), name=name)


def _in_proj_kernel(x_ref, g_ref, w_ref, b_ref, oa_ref, ob_ref, oc_ref, od_ref, og_ref):
    h = _rms_rows(x_ref[...], g_ref[_G_PRE:_G_PRE + 1, :]).astype(BF16)
    off = 0
    for o_ref, width in ((oa_ref, SEG_A), (ob_ref, SEG_B), (oc_ref, SEG_C), (od_ref, SEG_D), (og_ref, SEG_G)):
        acc = _nn(h, w_ref[:, off:off + width]) + b_ref[:, off:off + width]
        o_ref[...] = acc.astype(o_ref.dtype)
        off += width


def _in_proj(x2, gains, w, b, layer):
    m, d = x2.shape
    tm = min(ROW_TILE, m)
    widths = (SEG_A, SEG_B, SEG_C, SEG_D, SEG_G)
    dtypes = (BF16, BF16, BF16, BF16, F32)
    return _layered_call(
        _in_proj_kernel,
        grid=(m // tm,),
        in_specs=[_row_spec((tm, d)), _layer_spec((_G_ROWS, d)), _layer_spec((d, N_PROJ)), _layer_spec((1, N_PROJ))],
        out_specs=[_row_spec((tm, wd)) for wd in widths],
        out_shape=[jax.ShapeDtypeStruct((m, wd), dt) for wd, dt in zip(widths, dtypes)],
        name="in_proj",
    )(layer, x2, gains, w, b)


def _mlstm_kernel(a_ref, g_ref, conv_ref, ng_ref, o_ref,
                  q_s, k_s, bc_s, gt_s, bt_s, cl_s, nl_s, ml_s, ci_s, ni_s, mi_s):
    s_len = a_ref.shape[0]
    L = MLSTM_CHUNK
    nc = s_len // L
    grp = MLSTM_GROUP
    gw = GROUP_WIDTH

    rows_of = lambda c: pl.ds(pl.multiple_of(c * L, L), L)

    def conv_and_gates(c, carry):
        r0 = pl.multiple_of(c * L, L)
        halo = a_ref[pl.ds(pl.multiple_of(jnp.maximum(r0 - BF16_ROWS, 0), BF16_ROWS), BF16_ROWS), :2 * gw]
        halo = halo.astype(F32) * jnp.where(r0 > 0, 1.0, 0.0)
        qk = jnp.concatenate([halo, a_ref[rows_of(c), :2 * gw].astype(F32)], axis=0)
        acc = qk * conv_ref[CONV_WIDTH_A - 1:CONV_WIDTH_A, :]
        for j in range(1, CONV_WIDTH_A):
            acc = acc + pltpu.roll(qk, shift=j, axis=0) * conv_ref[CONV_WIDTH_A - 1 - j:CONV_WIDTH_A - j, :]
        acc = _silu(acc[BF16_ROWS:, :])
        q_s[rows_of(c), :] = acc[:, :gw].astype(BF16)
        k_s[rows_of(c), :] = (acc[:, gw:] * (HEAD_DIM ** -0.5)).astype(BF16)
        gates = g_ref[rows_of(c), :]
        bcum = _cumsum_rows(_log_sigmoid(gates), L)
        bc_s[rows_of(c), :] = bcum
        gt_s[c] = jnp.transpose(gates)
        bt_s[c] = jnp.transpose(bcum)
        return carry

    lax.fori_loop(0, nc, conv_and_gates, 0)

    ones_bd = _block_ones(gw)
    ones_rows = jnp.where((_row((N_HEADS * L, gw)) // L) == (_lane((N_HEADS * L, gw)) >> 6), 1.0, 0.0).astype(BF16)
    tril = _lane((L, L)) <= _row((L, L))
    bd_mask = (_row((gw, gw)) >> 6) == (_lane((gw, gw)) >> 6)
    rows8 = lambda u: jnp.broadcast_to(u, (SUBLANES, u.shape[-1]))

    def local_state(i, carry):
        cs = [i * grp + j for j in range(grp)]
        ks = [k_s[rows_of(c), :] for c in cs]
        e_fulls, ves = [], []
        for c in cs:
            v = a_ref[rows_of(c), 2 * gw:3 * gw]
            ig = pltpu.roll(g_ref[rows_of(c), :], shift=GATE_F - GATE_I, axis=1)
            w_loc = bc_s[pl.ds(pl.multiple_of(c * L, L) + L - 1, 1), :] - bc_s[rows_of(c), :] + ig
            m_loc = jnp.max(w_loc, axis=0, keepdims=True)
            ml_s[c] = rows8(m_loc)
            e_full = _per_head(_head_cols(jnp.exp(w_loc - m_loc), GATE_F), (L, gw))
            e_fulls.append(e_full)
            ves.append((v.astype(F32) * e_full).astype(BF16))
        c_locs = [_tn(ve, k) for ve, k in zip(ves, ks)]
        for c, c_loc, e_full, k in zip(cs, c_locs, e_fulls, ks):
            cl_s[c] = jnp.where(bd_mask, c_loc, 0.0)
            nl_s[c] = rows8(jnp.sum(e_full * k.astype(F32), axis=0, keepdims=True))
        return carry

    lax.fori_loop(0, nc // grp, local_state, 0)

    def scan(c, carry):
        c_prev, n_prev, m_prev = carry
        ci_s[c] = c_prev.astype(BF16)
        ni_s[c] = rows8(n_prev)
        mi_s[c] = rows8(m_prev)
        g_tot = bc_s[pl.ds(pl.multiple_of(c * L, L) + L - 1, 1), :]
        m_loc = ml_s[c][0:1, :]
        m_new = jnp.maximum(g_tot + m_prev, m_loc)
        a_row = jnp.exp(g_tot + m_prev - m_new)
        b_row = jnp.exp(m_loc - m_new)
        a_h, b_h = _head_cols(a_row, GATE_F), _head_cols(b_row, GATE_F)
        c_new = _per_head_rows(a_h, (gw, gw)) * c_prev + _per_head_rows(b_h, (gw, gw)) * cl_s[c]
        n_new = _per_head(a_h, (1, gw)) * n_prev + _per_head(b_h, (1, gw)) * nl_s[c][0:1, :]
        return c_new, n_new, m_new

    lax.fori_loop(0, nc, scan, (jnp.zeros((gw, gw), F32), jnp.zeros((1, gw), F32), jnp.zeros((1, LANES), F32)))

    def outputs(i, carry):
        cs = [i * grp + j for j in range(grp)]
        hid = _lane((L, gw)) >> 6
        qs = [q_s[rows_of(c), :] for c in cs]
        ks = [k_s[rows_of(c), :] for c in cs]
        vs = [a_ref[rows_of(c), 2 * gw:3 * gw] for c in cs]
        logits = [[_nt(jnp.where(hid == h, q, jnp.zeros_like(q)), k) for h in range(N_HEADS)] for q, k in zip(qs, ks)]
        num_inter = [_nt(q, ci_s[c]) for q, c in zip(qs, cs)]
        den_inter = [_group_sum(q.astype(F32) * ni_s[c][0:1, :], ones_bd) for q, c in zip(qs, cs)]
        shape = (L, gw)
        s_cat, iws, floors = [], [], []
        for j, c in enumerate(cs):
            b_col = bc_s[rows_of(c), :]
            g_row = gt_s[c]
            b_row = bt_s[c]
            ig = pltpu.roll(g_ref[rows_of(c), :], shift=GATE_F - GATE_I, axis=1)
            m_inter = b_col + mi_s[c][0:1, :]
            m_t = jnp.maximum(m_inter, b_col + _cummax_rows(ig - b_col, L))
            iws.append(_per_head(_head_cols(jnp.exp(m_inter - m_t), GATE_F), shape))
            floors.append(_per_head(_head_cols(jnp.exp(-m_t), GATE_F), shape))
            col_arg = b_col - m_t
            s_list = []
            for h in range(N_HEADS):
                row_arg = g_row[GATE_I + h:GATE_I + h + 1, :] - b_row[GATE_F + h:GATE_F + h + 1, :]
                arg = jnp.where(tril, col_arg[:, GATE_F + h:GATE_F + h + 1] + row_arg, -jnp.inf)
                s_list.append((logits[j][h] * jnp.exp(arg)).astype(BF16))
            s_cat.append(jnp.concatenate(s_list, axis=1))
        num_intra = [_nn(s, _stack_heads(v)) for s, v in zip(s_cat, vs)]
        den_intra = [_nn(s, ones_rows) for s in s_cat]
        hhs = []
        for j in range(grp):
            num = num_intra[j] + iws[j] * num_inter[j]
            den = den_intra[j] + iws[j] * den_inter[j]
            hhs.append(num / jnp.maximum(jnp.abs(den), floors[j]))
        mss = [_group_sum(hh * hh, ones_bd) for hh in hhs]
        for c, hh, ms in zip(cs, hhs, mss):
            z = a_ref[rows_of(c), 3 * gw:4 * gw].astype(F32)
            y = hh * lax.rsqrt(ms * (1.0 / HEAD_DIM) + NORM_EPS) * ng_ref[...] * _silu(z)
            o_ref[rows_of(c), :] = y.astype(o_ref.dtype)
        return carry

    lax.fori_loop(0, nc // grp, outputs, 0)


def _mlstm(a, gates, conv, ng, layer):
    b, s, _ = a.shape
    gw = GROUP_WIDTH
    L = MLSTM_CHUNK
    nc = s // L
    assert nc % MLSTM_GROUP == 0
    return _layered_call(
        _mlstm_kernel,
        grid=(b,),
        in_specs=[_row_spec((None, s, SEG_A)), _row_spec((None, s, SEG_G)),
                  _layer_spec((CONV_WIDTH_A, 2 * gw)), _layer_spec((1, gw))],
        out_specs=_row_spec((None, s, gw)),
        out_shape=jax.ShapeDtypeStruct((b, s, gw), BF16),
        scratch_shapes=[
            pltpu.VMEM((s, gw), BF16), pltpu.VMEM((s, gw), BF16),
            pltpu.VMEM((s, SEG_G), F32),
            pltpu.VMEM((nc, SEG_G, L), F32), pltpu.VMEM((nc, SEG_G, L), F32),
            pltpu.VMEM((nc, gw, gw), F32), pltpu.VMEM((nc, SUBLANES, gw), F32), pltpu.VMEM((nc, SUBLANES, LANES), F32),
            pltpu.VMEM((nc, gw, gw), BF16), pltpu.VMEM((nc, SUBLANES, gw), F32), pltpu.VMEM((nc, SUBLANES, LANES), F32),
        ],
        name="mlstm",
    )(layer, a, gates, conv, ng)


def _pool_kernel(b_ref, w_ref, sc_ref, o_ref):
    gw = GROUP_WIDTH
    u = b_ref[:, :gw].astype(F32)
    z = b_ref[:, gw:].astype(F32)
    pos = _row(u.shape).astype(F32)
    grp = _lane(u.shape) >> 6
    pooled = jnp.zeros_like(u)
    run, width = u, 1
    for gi, win in enumerate(POOL_WINDOWS):
        while width < win:
            run = run + _shift_rows(run, width)
            width *= 2
        count = jnp.minimum(pos + 1.0, float(win))
        pooled = jnp.where(grp == gi, run / count - u, pooled)
    mixed = _nn(pooled.astype(BF16), w_ref[...])
    o_ref[...] = (mixed * sc_ref[...] * _silu(z)).astype(o_ref.dtype)


def _pool(bx, w_bd, scale, layer):
    b, s, _ = bx.shape
    gw = GROUP_WIDTH
    return _layered_call(
        _pool_kernel,
        grid=(b,),
        in_specs=[_row_spec((None, s, SEG_B)), _layer_spec((gw, gw)), _layer_spec((1, gw))],
        out_specs=_row_spec((None, s, gw)),
        out_shape=jax.ShapeDtypeStruct((b, s, gw), BF16),
        name="pool",
    )(layer, bx, w_bd, scale)


def _cat_mm(xs, ys):
    return [_nn(x.astype(BF16), _stack_heads(y.astype(BF16))) for x, y in zip(xs, ys)]


def _unit_lower_inverse(mats):
    t = RWKV_CHUNK
    shift = RWKV_SUB.bit_length() - 1
    shape = mats[0].shape
    rb = _row(shape) >> shift
    col = _lane(shape) & (t - 1)
    cb = col >> shift
    eye = jnp.where(col == _row(shape), 1.0, 0.0)
    add = lambda xs, ys: [x + y for x, y in zip(xs, ys)]
    d = [jnp.where(rb == cb, a, 0.0) for a in mats]
    e = [jnp.where(rb == cb, 0.0, a) for a in mats]
    d2 = _cat_mm(d, d)
    d4 = _cat_mm(d2, d2)
    x = [eye + di for di in d]
    x = add(x, _cat_mm(x, d2))
    d8 = _cat_mm(d4, d4)
    x = add(x, _cat_mm(x, d4))
    dinv = add(x, _cat_mm(x, d8))
    f = _cat_mm(dinv, e)
    f2 = _cat_mm(f, f)
    w = [eye + fi for fi in f]
    w = add(w, _cat_mm(w, f2))
    return _cat_mm(w, dinv)


def _rwkv_kernel(c_ref, p_ref, w2a2_ref, o_ref,
                 v_s, bonus_s, pr_s, qk_s, ul_s, ar_s, ge_s, ub_s, rs_s):
    n_seq, s_len = c_ref.shape[0], c_ref.shape[1]
    gw = GROUP_WIDTH
    T = RWKV_CHUNK
    grp = RWKV_GROUP
    rows = T * grp
    nc = s_len // T
    prm = lambda i: p_ref[i:i + 1, :]

    ones_bd = _block_ones(gw)
    col = _lane((T, N_HEADS * T)) & (T - 1)
    row = _row((T, N_HEADS * T))
    strict = col < row
    incl = col <= row
    bd_mask = (_row((gw, gw)) >> 6) == (_lane((gw, gw)) >> 6)
    first_row = _row((rows, gw)) == 0

    def prepare(e, i, carry):
        r0 = pl.multiple_of(i * rows, rows)
        tile = pl.ds(r0, rows)
        halo = c_ref[e, pl.ds(pl.multiple_of(jnp.maximum(r0 - BF16_ROWS, 0), BF16_ROWS), BF16_ROWS), :]
        prev = halo[BF16_ROWS - 1:BF16_ROWS, :].astype(F32) * jnp.where(r0 > 0, 1.0, 0.0)

        def shifted(lo, hi, mu):
            u = c_ref[e, tile, lo:hi].astype(F32)
            before = jnp.where(first_row[:, :hi - lo], prev[:, lo:hi], pltpu.roll(u, shift=1, axis=0))
            return u + mu * (before - u)

        wa = shifted(4 * gw, 4 * gw + 2 * LORA, p_ref[_P_MU_WA:_P_MU_WA + 1, :2 * LORA])
        wa = jnp.where(_lane(wa.shape) < LORA, jnp.tanh(wa), wa)
        lora = _nn(wa.astype(BF16), w2a2_ref[...])
        ld = -DECAY_SCALE * _sigmoid(prm(_P_W0) + lora[:, :gw])
        a = _sigmoid(prm(_P_A0) + lora[:, gw:])
        r = shifted(0, gw, prm(_P_MU_R))
        k = shifted(gw, 2 * gw, prm(_P_MU_K))
        v = shifted(2 * gw, 3 * gw, prm(_P_MU_V))
        kk = k * prm(_P_KEY_K)
        kk = kk / jnp.maximum(jnp.sqrt(_group_sum(kk * kk, ones_bd)), 1e-12)
        k = k * (1.0 + (a - 1.0) * prm(_P_KEY_A))
        bonus_s[e, tile, :] = (_group_sum(r * k * prm(_P_BONUS), ones_bd) * v).astype(BF16)
        vb = v.astype(BF16)
        v_s[e, tile, :] = vb

        lc = _cumsum_rows(ld, T)
        g_inc = jnp.exp(lc)
        g_inv = jnp.exp(-lc)
        pt = (-kk * jnp.exp(lc - ld)).astype(BF16)
        rt = (r * g_inc).astype(BF16)
        qt = (kk * a * g_inv).astype(BF16)
        kt = (k * g_inv).astype(BF16)

        part = lambda u, j: u[j * T:(j + 1) * T, :]
        chunks = range(grp)
        prs = [jnp.concatenate([part(pt, j), part(rt, j)], axis=0) for j in chunks]
        gs = [_nt(prs[j], jnp.concatenate([_stack_heads(part(qt, j)), _stack_heads(part(kt, j))], axis=0))
              for j in chunks]
        a_pq = [jnp.where(strict, g[:T, :N_HEADS * T], 0.0) for g in gs]
        a_pk = [jnp.where(strict, g[:T, N_HEADS * T:], 0.0).astype(BF16) for g in gs]
        apv = [_nn(a_pk[j], _stack_heads(part(vb, j))) for j in chunks]
        tinv = [t.astype(BF16) for t in _unit_lower_inverse(a_pq)]
        p_fold = [_nn(tinv[j], _stack_heads(part(pt, j))) for j in chunks]
        u_loc = [_nn(tinv[j], _stack_heads(apv[j].astype(BF16))) for j in chunks]
        for j in chunks:
            c = i * grp + j
            pr_s[e, c] = jnp.concatenate([p_fold[j].astype(BF16), part(rt, j)], axis=0)
            qk_s[e, c] = jnp.concatenate([part(qt, j), part(kt, j)], axis=0)
            ul_s[e, c] = u_loc[j].astype(BF16)
            ar_s[e, c] = jnp.concatenate([jnp.where(incl, gs[j][T:, :N_HEADS * T], 0.0),
                                          jnp.where(incl, gs[j][T:, N_HEADS * T:], 0.0)], axis=1).astype(BF16)
            ge_s[e, c] = jnp.broadcast_to(part(g_inc, j)[T - 1:T, :], (SUBLANES, gw))
        return carry

    for e in range(n_seq):
        lax.fori_loop(0, nc // grp, functools.partial(prepare, e), 0)

    seqs = range(n_seq)

    def step(c, states):
        sl = pl.ds(pl.multiple_of(c * T, T), T)
        prs = [_nt(pr_s[e, c], states[e].astype(BF16)) for e in seqs]
        ubs = [(prs[e][:T] + ul_s[e, c].astype(F32)).astype(BF16) for e in seqs]
        dss = [_tn(jnp.concatenate([ubs[e], v_s[e, sl, :]], axis=0), qk_s[e, c]) for e in seqs]
        for e in seqs:
            ub_s[e, c] = ubs[e]
            rs_s[e, c] = prs[e][T:].astype(BF16)
        return tuple((states[e] + jnp.where(bd_mask, dss[e], 0.0)) * ge_s[e, c][0:1, :] for e in seqs)

    lax.fori_loop(0, nc, step, tuple(jnp.zeros((gw, gw), F32) for _ in seqs))

    def outputs(i, carry):
        items = [(e, i * RWKV_OUT_GROUP + j) for j in range(RWKV_OUT_GROUP) for e in seqs]
        sls = [pl.ds(pl.multiple_of(c * T, T), T) for _, c in items]
        ys = [rs_s[e, c].astype(F32)
              + _nn(ar_s[e, c], jnp.concatenate([_stack_heads(ub_s[e, c]), _stack_heads(v_s[e, sl, :])], axis=0))
              for (e, c), sl in zip(items, sls)]
        means = [_group_sum(y, ones_bd) * (1.0 / HEAD_DIM) for y in ys]
        ycs = [y - m for y, m in zip(ys, means)]
        vrs = [_group_sum(yc * yc, ones_bd) * (1.0 / HEAD_DIM) for yc in ycs]
        for (e, c), sl, yc, vr in zip(items, sls, ycs, vrs):
            y = yc * lax.rsqrt(vr + GN_EPS) * prm(_P_GN_G) + prm(_P_GN_B) + bonus_s[e, sl, :].astype(F32)
            z = c_ref[e, sl, 3 * gw:4 * gw].astype(F32)
            o_ref[e, sl, :] = (y * _silu(z)).astype(o_ref.dtype)
        return carry

    lax.fori_loop(0, nc // RWKV_OUT_GROUP, outputs, 0)


def _rwkv(cx, prm, w2a2, layer):
    b, s, _ = cx.shape
    gw = GROUP_WIDTH
    T = RWKV_CHUNK
    nc = s // T
    n_seq = RWKV_PAIR if b % RWKV_PAIR == 0 else 1
    assert nc % RWKV_GROUP == 0
    per_chunk = lambda r, w, dt: pltpu.VMEM((n_seq, nc, r, w), dt)
    return _layered_call(
        _rwkv_kernel,
        grid=(b // n_seq,),
        in_specs=[_row_spec((n_seq, s, SEG_C)), _layer_spec((_P_ROWS, gw)), _layer_spec((2 * LORA, 2 * gw))],
        out_specs=_row_spec((n_seq, s, gw)),
        out_shape=jax.ShapeDtypeStruct((b, s, gw), BF16),
        scratch_shapes=[
            pltpu.VMEM((n_seq, s, gw), BF16), pltpu.VMEM((n_seq, s, gw), BF16),
            per_chunk(2 * T, gw, BF16), per_chunk(2 * T, gw, BF16),
            per_chunk(T, gw, BF16),
            per_chunk(T, 2 * N_HEADS * T, BF16), per_chunk(SUBLANES, gw, F32),
            per_chunk(T, gw, BF16), per_chunk(T, gw, BF16),
        ],
        name="rwkv",
    )(layer, cx, prm, w2a2)


def _fox_kernel(d_ref, g_ref, qkg_ref, o_ref, q_s, k_s, vh_s, ft_s):
    s_len = d_ref.shape[0]
    gw = GROUP_WIDTH
    tq = min(FOX_QBLOCK, s_len)
    ones_bd = _block_ones(gw)

    def head_norm(u, g):
        ms = _group_sum(u * u, ones_bd) * (1.0 / HEAD_DIM)
        return u * lax.rsqrt(ms + NORM_EPS) * g

    q_s[...] = (head_norm(d_ref[:, 0:gw].astype(F32), qkg_ref[0:1, :]) * (HEAD_DIM ** -0.5)).astype(BF16)
    k_s[...] = head_norm(d_ref[:, gw:2 * gw].astype(F32), qkg_ref[1:2, :]).astype(BF16)
    v = d_ref[:, 2 * gw:3 * gw]
    den_lane = [((h + 1) % N_HEADS) * HEAD_DIM for h in range(N_HEADS)]
    for h in range(N_HEADS):
        spare = jnp.where(_lane(v.shape) == den_lane[h], 1.0, 0.0).astype(BF16)
        vh_s[h] = jnp.where((_lane(v.shape) >> 6) == h, v, spare)
    ft_s[...] = jnp.transpose(_cumsum_rows(_log_sigmoid(g_ref[...]), s_len))

    tril = _lane((tq, tq)) <= _row((tq, tq))
    hid = _lane((tq, gw)) >> 6
    for blk in range(s_len // tq):
        q0, q1 = blk * tq, (blk + 1) * tq
        q = q_s[q0:q1, :]
        out = jnp.zeros((tq, gw), F32)
        for h in range(N_HEADS):
            qh = jnp.where(hid == h, q, jnp.zeros_like(q))
            f_row = ft_s[GATE_D + h:GATE_D + h + 1, :]
            l_diag = jnp.where(tril, _nt(qh, k_s[q0:q1, :]) - f_row[:, q0:q1], -jnp.inf)
            mx = jnp.max(l_diag, axis=-1, keepdims=True)
            if blk:
                l_off = _nt(qh, k_s[0:q0, :]) - f_row[:, 0:q0]
                mx = jnp.maximum(mx, jnp.max(l_off, axis=-1, keepdims=True))
            acc = _nn(jnp.exp(l_diag - mx).astype(BF16), vh_s[h, q0:q1, :])
            if blk:
                acc = acc + _nn(jnp.exp(l_off - mx).astype(BF16), vh_s[h, 0:q0, :])
            out = jnp.where(hid == h, acc * (1.0 / acc[:, den_lane[h]:den_lane[h] + 1]), out)
        z = d_ref[q0:q1, 3 * gw:4 * gw].astype(F32)
        o_ref[q0:q1, :] = (out * _silu(z)).astype(o_ref.dtype)


def _fox(dx, gates, qkg, layer):
    b, s, _ = dx.shape
    gw = GROUP_WIDTH
    return _layered_call(
        _fox_kernel,
        grid=(b,),
        in_specs=[_row_spec((None, s, SEG_D)), _row_spec((None, s, SEG_G)), _layer_spec((2, gw))],
        out_specs=_row_spec((None, s, gw)),
        out_shape=jax.ShapeDtypeStruct((b, s, gw), BF16),
        scratch_shapes=[pltpu.VMEM((s, gw), BF16), pltpu.VMEM((s, gw), BF16),
                        pltpu.VMEM((N_HEADS, s, gw), BF16), pltpu.VMEM((SEG_G, s), F32)],
        name="fox",
    )(layer, dx, gates, qkg)


def _mem_kv_kernel(mem_ref, g_ref, w_ref, o_ref):
    mn = _rms_rows(mem_ref[...], g_ref[...]).astype(BF16)
    o_ref[...] = _nn(mn, w_ref[...]).astype(o_ref.dtype)


def _mem_kv(mem, g, wkv):
    b, n_mem, d = mem.shape
    depth = wkv.shape[0]
    return pl.pallas_call(
        _mem_kv_kernel,
        grid=(depth, b),
        in_specs=[
            pl.BlockSpec((None, n_mem, d), lambda l, i: (i, 0, 0)),
            pl.BlockSpec((1, d), lambda l, i: (0, 0)),
            pl.BlockSpec((None, d, 2 * d), lambda l, i: (l, 0, 0)),
        ],
        out_specs=pl.BlockSpec((None, None, n_mem, 2 * d), lambda l, i: (l, i, 0, 0)),
        out_shape=jax.ShapeDtypeStruct((depth, b, n_mem, 2 * d), BF16),
        compiler_params=_params(),
        name="mem_kv",
    )(mem, g, wkv)


def _out_xattn_kernel(x_ref, ya_ref, yb_ref, yc_ref, yd_ref, wout_ref, g_ref, wq_ref, kv_ref, wo_ref,
                      o_ref, *, n_mem_heads):
    d = x_ref.shape[-1]
    dh = d // n_mem_heads
    gain = lambda i: g_ref[i:i + 1, :]
    y = jnp.concatenate([ya_ref[...], yb_ref[...], yc_ref[...], yd_ref[...]], axis=-1)
    x = x_ref[...] + _rms_rows(_nn(y, wout_ref[...]), gain(_G_POST))
    hm = _rms_rows(x, gain(_G_XPRE)).astype(BF16)
    q = _nn(hm, wq_ref[...]).astype(BF16)
    outs = []
    for h in range(n_mem_heads):
        kh = kv_ref[:, h * dh:(h + 1) * dh]
        vh = kv_ref[:, d + h * dh:d + (h + 1) * dh]
        logits = _nt(q[:, h * dh:(h + 1) * dh], kh) * (dh ** -0.5)
        mx = jnp.max(logits, axis=-1, keepdims=True)
        p = jnp.exp(logits - mx)
        p = p / jnp.sum(p, axis=-1, keepdims=True)
        outs.append(_nn(p.astype(BF16), vh).astype(BF16))
    o = _nn(jnp.concatenate(outs, axis=-1), wo_ref[...])
    o_ref[...] = x + _rms_rows(o, gain(_G_XPOST))


def _out_xattn(x2, ys, wout, gains, wq, kv, wo, seq_len, layer, n_mem_heads=4):
    m, d = x2.shape
    gw = GROUP_WIDTH
    tm = min(ROW_TILE, seq_len)
    per_seq = seq_len // tm
    n_mem = kv.shape[2]
    return _layered_call(
        functools.partial(_out_xattn_kernel, n_mem_heads=n_mem_heads),
        grid=(m // tm,),
        in_specs=[_row_spec((tm, d))] + [_row_spec((tm, gw))] * 4 + [
            _layer_spec((4 * gw, d)),
            _layer_spec((_G_ROWS, d)),
            _layer_spec((d, d)),
            pl.BlockSpec((None, None, n_mem, 2 * d), lambda i, layer: (layer[0], i // per_seq, 0, 0)),
            _layer_spec((d, d)),
        ],
        out_specs=_row_spec((tm, d)),
        out_shape=jax.ShapeDtypeStruct((m, d), F32),
        name="out_xattn",
    )(layer, x2, *ys, wout, gains, wq, kv, wo)


def _split_in(w):
    gw = GROUP_WIDTH
    sizes = (("a_q", gw), ("a_k", gw), ("a_v", gw), ("a_i", N_HEADS), ("a_f", N_HEADS), ("a_z", gw),
             ("b_x", gw), ("b_z", gw),
             ("c_r", gw), ("c_k", gw), ("c_v", gw), ("c_w", LORA), ("c_a", LORA), ("c_z", gw),
             ("d_q", gw), ("d_k", gw), ("d_v", gw), ("d_f", N_HEADS), ("d_z", gw))
    out, off = {}, 0
    for name, size in sizes:
        out[name] = w[..., off:off + size]
        off += size
    return out


def _relayout_in(w):
    p = _split_in(w)
    pad = jnp.zeros(w.shape[:-1] + (SEG_G - 3 * N_HEADS,), w.dtype)
    order = ("a_q", "a_k", "a_v", "a_z", "b_x", "b_z", "c_r", "c_k", "c_v", "c_z", "c_w", "c_a",
             "d_q", "d_k", "d_v", "d_z", "a_i", "a_f", "d_f")
    return jnp.concatenate([p[n] for n in order] + [pad], axis=-1)


def _block_diag(blocks):
    n, k, _ = blocks.shape
    eye = jnp.eye(n, dtype=blocks.dtype)
    return jnp.einsum("gcd,gh->gchd", blocks, eye).reshape(n * k, n * k)


def kernel(x, mem, pre_norm_g, post_norm_g, w_in, b_in, conv_a, norm_a_g, pool_w, pool_scale, shift_mu_c,
           decay_w0, decay_w2, iclr_a0, iclr_a2, key_k, key_a, bonus_u, gn_c_g, gn_c_b, qk_norm_d, w_out,
           mem_norm_g, xattn_pre_g, xattn_post_g, xattn_wq, xattn_wkv, xattn_wo):
    bsz, seq, d = x.shape
    depth = w_in.shape[0]
    gw = GROUP_WIDTH

    w_in_r = _relayout_in(w_in.astype(BF16))
    b_in_r = _relayout_in(b_in).reshape(depth, 1, N_PROJ)
    gains = jnp.stack([pre_norm_g, post_norm_g, xattn_pre_g, xattn_post_g]
                      + [jnp.zeros_like(pre_norm_g)] * (_G_ROWS - 4), axis=1)
    mu = shift_mu_c
    zeros_row = jnp.zeros((depth, gw), F32)
    mu_wa = jnp.concatenate([mu[:, 3 * gw:], jnp.zeros((depth, gw - 2 * LORA), F32)], axis=-1)
    rows = [mu[:, 0:gw], mu[:, gw:2 * gw], mu[:, 2 * gw:3 * gw], decay_w0, iclr_a0, key_k, key_a, bonus_u,
            gn_c_g, gn_c_b, mu_wa] + [zeros_row] * (_P_ROWS - 11)
    rwkv_prm = jnp.stack(rows, axis=1)
    zl = jnp.zeros((depth, LORA, gw), F32)
    w2a2 = jnp.concatenate([jnp.concatenate([decay_w2, zl], axis=-1),
                            jnp.concatenate([zl, iclr_a2], axis=-1)], axis=1).astype(BF16)
    pool_bd = jax.vmap(_block_diag)(pool_w).astype(BF16)
    qkg = jnp.tile(qk_norm_d, (1, 1, N_HEADS))
    norm_a = norm_a_g.reshape(depth, 1, gw)
    pool_sc = pool_scale.reshape(depth, 1, gw)
    w_out_b = w_out.astype(BF16)
    wq_b = xattn_wq.astype(BF16)
    wo_b = xattn_wo.astype(BF16)

    kv_all = _mem_kv(mem, mem_norm_g.reshape(1, d), xattn_wkv.astype(BF16))

    x2 = x.reshape(bsz * seq, d)
    sh = lambda u: u.reshape(bsz, seq, u.shape[-1])
    for l in range(depth):
        layer = jnp.full((1,), l, jnp.int32)
        pa, pb, pc, pd, pg = _in_proj(x2, gains, w_in_r, b_in_r, layer)
        gates = sh(pg)
        y_a = _mlstm(sh(pa), gates, conv_a, norm_a, layer)
        y_b = _pool(sh(pb), pool_bd, pool_sc, layer)
        y_c = _rwkv(sh(pc), rwkv_prm, w2a2, layer)
        y_d = _fox(sh(pd), gates, qkg, layer)
        ys = [u.reshape(bsz * seq, gw) for u in (y_a, y_b, y_c, y_d)]
        x2 = _out_xattn(x2, ys, w_out_b, gains, wq_b, kv_all, wo_b, seq, layer)
    return x2.reshape(bsz, seq, d)
```

```python
import functools

import jax
import jax.numpy as jnp
from jax import lax
from jax.experimental import pallas as pl
from jax.experimental.pallas import tpu as pltpu

F32 = jnp.float32
BF16 = jnp.bfloat16

HEAD_DIM = 64
N_HEADS = 4
GROUP_WIDTH = HEAD_DIM * N_HEADS
LANES = 128
SUBLANES = 8
BF16_ROWS = 16
NORM_EPS = 1e-6
GN_EPS = 64e-5
LOG2_E = 1.4426950408889634
POOL_WINDOWS = (2, 4, 8, 16)
CONV_WIDTH_A = 4
LORA = 64
MLSTM_CHUNK = 128
MLSTM_GROUP = 4
RWKV_CHUNK = 64
RWKV_SUB = 16
RWKV_GROUP = 8
RWKV_PAIR = 2
DECAY_SCALE = 0.6065306597126334
FOX_QBLOCK = 256
ROW_TILE = 512
XATTN_ROW_TILE = 1024
XATTN_PARTS = 2
RELAYOUT_ROWS = 128
VMEM_LIMIT = 56 * 1024 * 1024

SEG_A = 4 * GROUP_WIDTH
SEG_B = 2 * GROUP_WIDTH
SEG_C = 4 * GROUP_WIDTH + 2 * LORA
SEG_D = 4 * GROUP_WIDTH
SEG_G = LANES
N_PROJ = SEG_A + SEG_B + SEG_C + SEG_D + SEG_G
GATE_I, GATE_F, GATE_D = 0, N_HEADS, 2 * N_HEADS

(_G_PRE, _G_POST, _G_XPRE, _G_XPOST) = range(4)
_G_ROWS = 8

(_P_MU_R, _P_MU_K, _P_MU_V, _P_W0, _P_A0, _P_KEY_K, _P_KEY_A, _P_BONUS, _P_GN_G, _P_GN_B, _P_MU_WA) = range(11)
_P_ROWS = 16


def _lane(shape):
    return lax.broadcasted_iota(jnp.int32, shape, len(shape) - 1)


def _row(shape):
    return lax.broadcasted_iota(jnp.int32, shape, len(shape) - 2)


def _nt(a, b):
    return lax.dot_general(a, b, (((1,), (1,)), ((), ())), preferred_element_type=F32)


def _tn(a, b):
    return lax.dot_general(a, b, (((0,), (0,)), ((), ())), preferred_element_type=F32)


def _nn(a, b):
    return jnp.dot(a, b, preferred_element_type=F32)


def _sigmoid(x):
    return 1.0 / (1.0 + jnp.exp(-x))


def _silu(x):
    return x * _sigmoid(x)


def _softplus(x):
    return jnp.maximum(x, 0.0) + jnp.log(1.0 + jnp.exp(-jnp.abs(x)))


def _log_sigmoid(x):
    return -_softplus(-x)


def _shift_rows(x, s):
    rolled = pltpu.roll(x, shift=s, axis=0)
    return jnp.where(_row(x.shape) >= s, rolled, 0.0)


def _cumsum_rows(x, seg):
    pos = _row(x.shape) & (seg - 1)
    s = 1
    while s < seg:
        x = x + jnp.where(pos >= s, pltpu.roll(x, shift=s, axis=0), 0.0)
        s *= 2
    return x


def _cummax_rows(x, seg):
    pos = _row(x.shape) & (seg - 1)
    s = 1
    while s < seg:
        x = jnp.maximum(x, jnp.where(pos >= s, pltpu.roll(x, shift=s, axis=0), -jnp.inf))
        s *= 2
    return x


def _stack_heads(x):
    hid = _lane(x.shape) >> 6
    zero = jnp.zeros_like(x)
    return jnp.concatenate([jnp.where(hid == h, x, zero) for h in range(N_HEADS)], axis=0)


def _per_head(cols, shape):
    hid = _lane(shape) >> 6
    out = jnp.broadcast_to(cols[N_HEADS - 1], shape)
    for h in range(N_HEADS - 2, -1, -1):
        out = jnp.where(hid == h, cols[h], out)
    return out


def _per_head_rows(vals, shape):
    hid = _row(shape) >> 6
    out = jnp.broadcast_to(vals[N_HEADS - 1], shape)
    for h in range(N_HEADS - 2, -1, -1):
        out = jnp.where(hid == h, vals[h], out)
    return out


def _head_cols(x, first):
    return [x[:, first + h:first + h + 1] for h in range(N_HEADS)]


def _block_ones(n):
    return jnp.where((_row((n, n)) >> 6) == (_lane((n, n)) >> 6), 1.0, 0.0).astype(BF16)


def _group_sum(x, ones_bd):
    hi = x.astype(BF16)
    r1 = x - hi.astype(F32)
    mid = r1.astype(BF16)
    lo = (r1 - mid.astype(F32)).astype(BF16)
    return _nn(hi, ones_bd) + _nn(mid, ones_bd) + _nn(lo, ones_bd)


def _rms_rows(x, g):
    return x * lax.rsqrt(jnp.mean(x * x, axis=-1, keepdims=True) + NORM_EPS) * g


def _params(**kw):
    return pltpu.CompilerParams(vmem_limit_bytes=VMEM_LIMIT, **kw)


def _layer_spec(shape):
    zeros = (0,) * len(shape)
    return pl.BlockSpec((None,) + tuple(shape), lambda i, layer: (layer[0],) + zeros)


def _row_spec(shape):
    zeros = (0,) * (len(shape) - 1)
    return pl.BlockSpec(tuple(shape), lambda i, layer: (i,) + zeros)


def _layered_call(body, grid, in_specs, out_specs, out_shape, scratch_shapes=(), name=None):
    def with_layer(layer_ref, *refs):
        del layer_ref
        body(*refs)

    return pl.pallas_call(
        with_layer,
        grid_spec=pltpu.PrefetchScalarGridSpec(num_scalar_prefetch=1, grid=grid, in_specs=in_specs,
                                               out_specs=out_specs, scratch_shapes=scratch_shapes),
        out_shape=out_shape, compiler_params=_params(), name=name)


def _in_proj_kernel(x_ref, g_ref, w_ref, b_ref, oa_ref, ob_ref, oc_ref, od_ref, og_ref):
    h = _rms_rows(x_ref[...], g_ref[_G_PRE:_G_PRE + 1, :]).astype(BF16)
    off = 0
    for o_ref, width in ((oa_ref, SEG_A), (ob_ref, SEG_B), (oc_ref, SEG_C), (od_ref, SEG_D), (og_ref, SEG_G)):
        acc = _nn(h, w_ref[:, off:off + width]) + b_ref[:, off:off + width]
        o_ref[...] = acc.astype(o_ref.dtype)
        off += width


def _in_proj(x2, gains, w, b, layer):
    m, d = x2.shape
    tm = min(ROW_TILE, m)
    widths = (SEG_A, SEG_B, SEG_C, SEG_D, SEG_G)
    dtypes = (BF16, BF16, BF16, BF16, F32)
    return _layered_call(
        _in_proj_kernel,
        grid=(m // tm,),
        in_specs=[_row_spec((tm, d)), _layer_spec((_G_ROWS, d)), _layer_spec((d, N_PROJ)), _layer_spec((1, N_PROJ))],
        out_specs=[_row_spec((tm, wd)) for wd in widths],
        out_shape=[jax.ShapeDtypeStruct((m, wd), dt) for wd, dt in zip(widths, dtypes)],
        name="in_proj",
    )(layer, x2, gains, w, b)


def _mlstm_kernel(a_ref, g_ref, conv_ref, ng_ref, o_ref,
                  q_s, k_s, bc_s, gt_s, bt_s, cl_s, nl_s, ml_s):
    s_len = a_ref.shape[0]
    L = MLSTM_CHUNK
    nc = s_len // L
    grp = MLSTM_GROUP
    gw = GROUP_WIDTH

    rows_of = lambda c: pl.ds(pl.multiple_of(c * L, L), L)

    def conv_and_gates(c, carry):
        r0 = pl.multiple_of(c * L, L)
        halo = a_ref[pl.ds(pl.multiple_of(jnp.maximum(r0 - BF16_ROWS, 0), BF16_ROWS), BF16_ROWS), :2 * gw]
        halo = halo.astype(F32) * jnp.where(r0 > 0, 1.0, 0.0)
        qk = jnp.concatenate([halo, a_ref[rows_of(c), :2 * gw].astype(F32)], axis=0)
        acc = qk * conv_ref[CONV_WIDTH_A - 1:CONV_WIDTH_A, :]
        for j in range(1, CONV_WIDTH_A):
            acc = acc + pltpu.roll(qk, shift=j, axis=0) * conv_ref[CONV_WIDTH_A - 1 - j:CONV_WIDTH_A - j, :]
        acc = _silu(acc[BF16_ROWS:, :])
        q_s[rows_of(c), :] = acc[:, :gw].astype(BF16)
        k_s[rows_of(c), :] = (acc[:, gw:] * (HEAD_DIM ** -0.5)).astype(BF16)
        gates = g_ref[rows_of(c), :]
        bcum = _cumsum_rows(_log_sigmoid(gates), L)
        bc_s[rows_of(c), :] = bcum
        gt_s[c] = jnp.transpose(gates)
        bt_s[c] = jnp.transpose(bcum)
        return carry

    ones_bd = _block_ones(gw)
    ones_rows = jnp.where((_row((N_HEADS * L, gw)) // L) == (_lane((N_HEADS * L, gw)) >> 6), 1.0, 0.0).astype(BF16)
    tril = _lane((L, L)) <= _row((L, L))
    bd_mask = (_row((gw, gw)) >> 6) == (_lane((gw, gw)) >> 6)
    rows8 = lambda u: jnp.broadcast_to(u, (SUBLANES, u.shape[-1]))

    def local_state(i, carry):
        cs = [i * grp + j for j in range(grp)]
        for c in cs:
            conv_and_gates(c, carry)
        ks = [k_s[rows_of(c), :] for c in cs]
        e_fulls, ves = [], []
        for c in cs:
            v = a_ref[rows_of(c), 2 * gw:3 * gw]
            ig = pltpu.roll(g_ref[rows_of(c), :], shift=GATE_F - GATE_I, axis=1)
            w_loc = bc_s[pl.ds(pl.multiple_of(c * L, L) + L - 1, 1), :] - bc_s[rows_of(c), :] + ig
            m_loc = jnp.max(w_loc, axis=0, keepdims=True)
            ml_s[c] = rows8(m_loc)
            e_full = _per_head(_head_cols(jnp.exp(w_loc - m_loc), GATE_F), (L, gw))
            e_fulls.append(e_full)
            ves.append((v.astype(F32) * e_full).astype(BF16))
        c_locs = [_tn(ve, k) for ve, k in zip(ves, ks)]
        for c, c_loc, e_full, k in zip(cs, c_locs, e_fulls, ks):
            cl_s[c] = jnp.where(bd_mask, c_loc, 0.0)
            nl_s[c] = rows8(jnp.sum(e_full * k.astype(F32), axis=0, keepdims=True))
        return carry

    lax.fori_loop(0, nc // grp, local_state, 0)

    def scan(c, carry):
        c_prev, n_prev, m_prev = carry
        g_tot = bc_s[pl.ds(pl.multiple_of(c * L, L) + L - 1, 1), :]
        m_loc = ml_s[c][0:1, :]
        m_new = jnp.maximum(g_tot + m_prev, m_loc)
        a_row = jnp.exp(g_tot + m_prev - m_new)
        b_row = jnp.exp(m_loc - m_new)
        a_h, b_h = _head_cols(a_row, GATE_F), _head_cols(b_row, GATE_F)
        c_new = _per_head_rows(a_h, (gw, gw)) * c_prev + _per_head_rows(b_h, (gw, gw)) * cl_s[c]
        n_new = _per_head(a_h, (1, gw)) * n_prev + _per_head(b_h, (1, gw)) * nl_s[c][0:1, :]
        return c_new, n_new, m_new

    def outputs(i, carry):
        cs = [i * grp + j for j in range(grp)]
        states = [carry]
        for c in cs:
            states.append(scan(c, states[-1]))
        hid = _lane((L, gw)) >> 6
        qs = [q_s[rows_of(c), :] for c in cs]
        ks = [k_s[rows_of(c), :] for c in cs]
        vs = [a_ref[rows_of(c), 2 * gw:3 * gw] for c in cs]
        logits = [[_nt(jnp.where(hid == h, q, jnp.zeros_like(q)), k) for h in range(N_HEADS)] for q, k in zip(qs, ks)]
        num_inter = [_nt(q, st[0].astype(BF16)) for q, st in zip(qs, states)]
        den_inter = [_group_sum(q.astype(F32) * st[1], ones_bd) for q, st in zip(qs, states)]
        shape = (L, gw)
        s_cat, iws, floors = [], [], []
        for j, c in enumerate(cs):
            b_col = bc_s[rows_of(c), :]
            g_row = gt_s[c]
            b_row = bt_s[c]
            ig = pltpu.roll(g_ref[rows_of(c), :], shift=GATE_F - GATE_I, axis=1)
            m_inter = b_col + states[j][2]
            m_t = jnp.maximum(m_inter, b_col + _cummax_rows(ig - b_col, L))
            iws.append(_per_head(_head_cols(jnp.exp(m_inter - m_t), GATE_F), shape))
            floors.append(_per_head(_head_cols(jnp.exp(-m_t), GATE_F), shape))
            col_arg = b_col - m_t
            s_list = []
            for h in range(N_HEADS):
                row_arg = g_row[GATE_I + h:GATE_I + h + 1, :] - b_row[GATE_F + h:GATE_F + h + 1, :]
                arg = jnp.where(tril, col_arg[:, GATE_F + h:GATE_F + h + 1] + row_arg, -jnp.inf)
                s_list.append((logits[j][h] * jnp.exp(arg)).astype(BF16))
            s_cat.append(jnp.concatenate(s_list, axis=1))
        num_intra = [_nn(s, _stack_heads(v)) for s, v in zip(s_cat, vs)]
        den_intra = [_nn(s, ones_rows) for s in s_cat]
        hhs = []
        for j in range(grp):
            num = num_intra[j] + iws[j] * num_inter[j]
            den = den_intra[j] + iws[j] * den_inter[j]
            hhs.append(num / jnp.maximum(jnp.abs(den), floors[j]))
        mss = [_group_sum(hh * hh, ones_bd) for hh in hhs]
        for c, hh, ms in zip(cs, hhs, mss):
            z = a_ref[rows_of(c), 3 * gw:4 * gw].astype(F32)
            y = hh * lax.rsqrt(ms * (1.0 / HEAD_DIM) + NORM_EPS) * ng_ref[...] * _silu(z)
            o_ref[rows_of(c), :] = y.astype(o_ref.dtype)
        return states[-1]

    lax.fori_loop(0, nc // grp, outputs,
                  (jnp.zeros((gw, gw), F32), jnp.zeros((1, gw), F32), jnp.zeros((1, LANES), F32)))


def _mlstm(a, gates, conv, ng, layer):
    b, s, _ = a.shape
    gw = GROUP_WIDTH
    L = MLSTM_CHUNK
    nc = s // L
    assert nc % MLSTM_GROUP == 0
    return _layered_call(
        _mlstm_kernel,
        grid=(b,),
        in_specs=[_row_spec((None, s, SEG_A)), _row_spec((None, s, SEG_G)),
                  _layer_spec((CONV_WIDTH_A, 2 * gw)), _layer_spec((1, gw))],
        out_specs=_row_spec((None, s, gw)),
        out_shape=jax.ShapeDtypeStruct((b, s, gw), BF16),
        scratch_shapes=[
            pltpu.VMEM((s, gw), BF16), pltpu.VMEM((s, gw), BF16),
            pltpu.VMEM((s, SEG_G), F32),
            pltpu.VMEM((nc, SEG_G, L), F32), pltpu.VMEM((nc, SEG_G, L), F32),
            pltpu.VMEM((nc, gw, gw), F32), pltpu.VMEM((nc, SUBLANES, gw), F32), pltpu.VMEM((nc, SUBLANES, LANES), F32),
        ],
        name="mlstm",
    )(layer, a, gates, conv, ng)


def _pool_kernel(b_ref, w_ref, sc_ref, o_ref):
    gw = GROUP_WIDTH
    u = b_ref[:, :gw].astype(F32)
    z = b_ref[:, gw:].astype(F32)
    pos = _row(u.shape).astype(F32)
    grp = _lane(u.shape) >> 6
    run, width = u, 1
    window = jnp.full(u.shape, float(POOL_WINDOWS[-1]), F32)
    sums = []
    for win in POOL_WINDOWS:
        while width < win:
            run = run + _shift_rows(run, width)
            width *= 2
        sums.append(run)
    window_sum = sums[-1]
    for gi in range(len(POOL_WINDOWS) - 2, -1, -1):
        window_sum = jnp.where(grp == gi, sums[gi], window_sum)
        window = jnp.where(grp == gi, float(POOL_WINDOWS[gi]), window)
    pooled = window_sum / jnp.minimum(pos + 1.0, window) - u
    mixed = _nn(pooled.astype(BF16), w_ref[...])
    o_ref[...] = (mixed * sc_ref[...] * _silu(z)).astype(o_ref.dtype)


def _pool(bx, w_bd, scale, layer):
    b, s, _ = bx.shape
    gw = GROUP_WIDTH
    return _layered_call(
        _pool_kernel,
        grid=(b,),
        in_specs=[_row_spec((None, s, SEG_B)), _layer_spec((gw, gw)), _layer_spec((1, gw))],
        out_specs=_row_spec((None, s, gw)),
        out_shape=jax.ShapeDtypeStruct((b, s, gw), BF16),
        name="pool",
    )(layer, bx, w_bd, scale)


def _cat_mm(xs, ys):
    return [_nn(x.astype(BF16), _stack_heads(y.astype(BF16))) for x, y in zip(xs, ys)]


def _unit_lower_inverse(mats):
    t = RWKV_CHUNK
    shift = RWKV_SUB.bit_length() - 1
    shape = mats[0].shape
    rb = _row(shape) >> shift
    col = _lane(shape) & (t - 1)
    cb = col >> shift
    eye = jnp.where(col == _row(shape), 1.0, 0.0)
    add = lambda xs, ys: [x + y for x, y in zip(xs, ys)]
    d = [jnp.where(rb == cb, a, 0.0) for a in mats]
    e = [jnp.where(rb == cb, 0.0, a) for a in mats]
    d2 = _cat_mm(d, d)
    d4 = _cat_mm(d2, d2)
    x = [eye + di for di in d]
    x = add(x, _cat_mm(x, d2))
    d8 = _cat_mm(d4, d4)
    x = add(x, _cat_mm(x, d4))
    dinv = add(x, _cat_mm(x, d8))
    f = _cat_mm(dinv, e)
    f2 = _cat_mm(f, f)
    w = [eye + fi for fi in f]
    w = add(w, _cat_mm(w, f2))
    return _cat_mm(w, dinv)


def _rwkv_kernel(c_ref, p_ref, w2a2_ref, o_ref,
                 v_s, bonus_s, pr_s, qk_s, ul_s, ar_s, ge_s, ub_s, rs_s):
    n_seq, s_len = c_ref.shape[0], c_ref.shape[1]
    gw = GROUP_WIDTH
    T = RWKV_CHUNK
    grp = RWKV_GROUP
    rows = T * grp
    nc = s_len // T
    prm = lambda i: p_ref[i:i + 1, :]

    ones_bd = _block_ones(gw)
    col = _lane((T, N_HEADS * T)) & (T - 1)
    row = _row((T, N_HEADS * T))
    strict = col < row
    incl = col <= row
    bd_mask = (_row((gw, gw)) >> 6) == (_lane((gw, gw)) >> 6)
    first_row = _row((rows, gw)) == 0

    def prepare(e, i, carry):
        r0 = pl.multiple_of(i * rows, rows)
        tile = pl.ds(r0, rows)
        halo = c_ref[e, pl.ds(pl.multiple_of(jnp.maximum(r0 - BF16_ROWS, 0), BF16_ROWS), BF16_ROWS), :]
        prev = halo[BF16_ROWS - 1:BF16_ROWS, :].astype(F32) * jnp.where(r0 > 0, 1.0, 0.0)

        def shifted(lo, hi, mu):
            u = c_ref[e, tile, lo:hi].astype(F32)
            before = jnp.where(first_row[:, :hi - lo], prev[:, lo:hi], pltpu.roll(u, shift=1, axis=0))
            return u + mu * (before - u)

        wa = shifted(4 * gw, 4 * gw + 2 * LORA, p_ref[_P_MU_WA:_P_MU_WA + 1, :2 * LORA])
        wa = jnp.where(_lane(wa.shape) < LORA, jnp.tanh(wa), wa)
        lora = _nn(wa.astype(BF16), w2a2_ref[...])
        ld = -DECAY_SCALE * _sigmoid(prm(_P_W0) + lora[:, :gw])
        a = _sigmoid(prm(_P_A0) + lora[:, gw:])
        r = shifted(0, gw, prm(_P_MU_R))
        k = shifted(gw, 2 * gw, prm(_P_MU_K))
        v = shifted(2 * gw, 3 * gw, prm(_P_MU_V))
        kk = k * prm(_P_KEY_K)
        kk = kk / jnp.maximum(jnp.sqrt(_group_sum(kk * kk, ones_bd)), 1e-12)
        k = k * (1.0 + (a - 1.0) * prm(_P_KEY_A))
        bonus_s[e, tile, :] = (_group_sum(r * k * prm(_P_BONUS), ones_bd) * v).astype(BF16)
        vb = v.astype(BF16)
        v_s[e, tile, :] = vb

        lc = _cumsum_rows(ld, T)
        g_inc = jnp.exp(lc)
        g_inv = jnp.exp(-lc)
        pt = (-kk * jnp.exp(lc - ld)).astype(BF16)
        rt = (r * g_inc).astype(BF16)
        qt = (kk * a * g_inv).astype(BF16)
        kt = (k * g_inv).astype(BF16)

        part = lambda u, j: u[j * T:(j + 1) * T, :]
        chunks = range(grp)
        prs = [jnp.concatenate([part(pt, j), part(rt, j)], axis=0) for j in chunks]
        gs = [_nt(prs[j], jnp.concatenate([_stack_heads(part(qt, j)), _stack_heads(part(kt, j))], axis=0))
              for j in chunks]
        a_pq = [jnp.where(strict, g[:T, :N_HEADS * T], 0.0) for g in gs]
        a_pk = [jnp.where(strict, g[:T, N_HEADS * T:], 0.0).astype(BF16) for g in gs]
        apv = [_nn(a_pk[j], _stack_heads(part(vb, j))) for j in chunks]
        tinv = [t.astype(BF16) for t in _unit_lower_inverse(a_pq)]
        p_fold = [_nn(tinv[j], _stack_heads(part(pt, j))) for j in chunks]
        u_loc = [_nn(tinv[j], _stack_heads(apv[j].astype(BF16))) for j in chunks]
        for j in chunks:
            c = i * grp + j
            pr_s[e, c] = jnp.concatenate([p_fold[j].astype(BF16), part(rt, j)], axis=0)
            qk_s[e, c] = jnp.concatenate([part(qt, j), part(kt, j)], axis=0)
            ul_s[e, c] = u_loc[j].astype(BF16)
            ar_s[e, c] = jnp.concatenate([jnp.where(incl, gs[j][T:, :N_HEADS * T], 0.0),
                                          jnp.where(incl, gs[j][T:, N_HEADS * T:], 0.0)], axis=1).astype(BF16)
            ge_s[e, c] = jnp.broadcast_to(part(g_inc, j)[T - 1:T, :], (SUBLANES, gw))
        return carry

    for e in range(n_seq):
        lax.fori_loop(0, nc // grp, functools.partial(prepare, e), 0)

    seqs = range(n_seq)

    rows_of = lambda c: pl.ds(pl.multiple_of(c * T, T), T)

    def project(c, states):
        return [_nt(pr_s[e, c], states[e].astype(BF16)) for e in seqs]

    def correct(c, prs):
        ubs = [(prs[e][:T] + ul_s[e, c].astype(F32)).astype(BF16) for e in seqs]
        dss = [_tn(jnp.concatenate([ubs[e], v_s[e, rows_of(c), :]], axis=0), qk_s[e, c]) for e in seqs]
        for e in seqs:
            ub_s[e, c] = ubs[e]
            rs_s[e, c] = prs[e][T:].astype(BF16)
        return dss

    def advance(c, states, dss):
        return tuple((states[e] + jnp.where(bd_mask, dss[e], 0.0)) * ge_s[e, c][0:1, :] for e in seqs)

    def readout(c):
        return [rs_s[e, c].astype(F32)
                + _nn(ar_s[e, c], jnp.concatenate([_stack_heads(ub_s[e, c]), _stack_heads(v_s[e, rows_of(c), :])], axis=0))
                for e in seqs]

    def centre(ys):
        means = [_group_sum(y, ones_bd) * (1.0 / HEAD_DIM) for y in ys]
        return [y - m for y, m in zip(ys, means)]

    def finish(c, ycs):
        vrs = [_group_sum(yc * yc, ones_bd) * (1.0 / HEAD_DIM) for yc in ycs]
        for e in seqs:
            y = ycs[e] * lax.rsqrt(vrs[e] + GN_EPS) * prm(_P_GN_G) + prm(_P_GN_B) + bonus_s[e, rows_of(c), :].astype(F32)
            z = c_ref[e, rows_of(c), 3 * gw:4 * gw].astype(F32)
            o_ref[e, rows_of(c), :] = (y * _silu(z)).astype(o_ref.dtype)

    def step(c, states):
        prs = project(c, states)
        ys = readout(c - 1)
        dss = correct(c, prs)
        ycs = centre(ys)
        new_states = advance(c, states, dss)
        finish(c - 1, ycs)
        return new_states

    zero_states = tuple(jnp.zeros((gw, gw), F32) for _ in seqs)
    states = advance(0, zero_states, correct(0, project(0, zero_states)))
    lax.fori_loop(1, nc, step, states, unroll=2)
    finish(nc - 1, centre(readout(nc - 1)))


def _rwkv(cx, prm, w2a2, layer):
    b, s, _ = cx.shape
    gw = GROUP_WIDTH
    T = RWKV_CHUNK
    nc = s // T
    n_seq = RWKV_PAIR if b % RWKV_PAIR == 0 else 1
    assert nc % RWKV_GROUP == 0
    per_chunk = lambda r, w, dt: pltpu.VMEM((n_seq, nc, r, w), dt)
    return _layered_call(
        _rwkv_kernel,
        grid=(b // n_seq,),
        in_specs=[_row_spec((n_seq, s, SEG_C)), _layer_spec((_P_ROWS, gw)), _layer_spec((2 * LORA, 2 * gw))],
        out_specs=_row_spec((n_seq, s, gw)),
        out_shape=jax.ShapeDtypeStruct((b, s, gw), BF16),
        scratch_shapes=[
            pltpu.VMEM((n_seq, s, gw), BF16), pltpu.VMEM((n_seq, s, gw), BF16),
            per_chunk(2 * T, gw, BF16), per_chunk(2 * T, gw, BF16),
            per_chunk(T, gw, BF16),
            per_chunk(T, 2 * N_HEADS * T, BF16), per_chunk(SUBLANES, gw, F32),
            per_chunk(T, gw, BF16), per_chunk(T, gw, BF16),
        ],
        name="rwkv",
    )(layer, cx, prm, w2a2)


def _fox_kernel(d_ref, g_ref, qkg_ref, o_ref, q_s, k_s, vh_s, ft_s):
    s_len = d_ref.shape[0]
    gw = GROUP_WIDTH
    tq = min(FOX_QBLOCK, s_len)
    ones_bd = _block_ones(gw)

    def head_norm(u, g):
        ms = _group_sum(u * u, ones_bd) * (1.0 / HEAD_DIM)
        return u * lax.rsqrt(ms + NORM_EPS) * g

    q_s[...] = (head_norm(d_ref[:, 0:gw].astype(F32), qkg_ref[0:1, :]) * (LOG2_E * HEAD_DIM ** -0.5)).astype(BF16)
    k_s[...] = head_norm(d_ref[:, gw:2 * gw].astype(F32), qkg_ref[1:2, :]).astype(BF16)
    v = d_ref[:, 2 * gw:3 * gw]
    den_lane = [((h + 1) % N_HEADS) * HEAD_DIM for h in range(N_HEADS)]
    for h in range(N_HEADS):
        spare = jnp.where(_lane(v.shape) == den_lane[h], 1.0, 0.0).astype(BF16)
        vh_s[h] = jnp.where((_lane(v.shape) >> 6) == h, v, spare)
    ft_s[...] = jnp.transpose(_cumsum_rows(_log_sigmoid(g_ref[...]), s_len) * LOG2_E)

    tril = _lane((tq, tq)) <= _row((tq, tq))
    hid = _lane((tq, gw)) >> 6
    for blk in range(s_len // tq):
        q0, q1 = blk * tq, (blk + 1) * tq
        q = q_s[q0:q1, :]
        out = jnp.zeros((tq, gw), F32)
        for h in range(N_HEADS):
            qh = jnp.where(hid == h, q, jnp.zeros_like(q))
            f_row = ft_s[GATE_D + h:GATE_D + h + 1, :]
            l_diag = jnp.where(tril, _nt(qh, k_s[q0:q1, :]) - f_row[:, q0:q1], -jnp.inf)
            mx = jnp.max(l_diag, axis=-1, keepdims=True)
            if blk:
                l_off = _nt(qh, k_s[0:q0, :]) - f_row[:, 0:q0]
                mx = jnp.maximum(mx, jnp.max(l_off, axis=-1, keepdims=True))
            acc = _nn(jnp.exp2(l_diag - mx).astype(BF16), vh_s[h, q0:q1, :])
            if blk:
                acc = acc + _nn(jnp.exp2(l_off - mx).astype(BF16), vh_s[h, 0:q0, :])
            out = jnp.where(hid == h, acc * (1.0 / acc[:, den_lane[h]:den_lane[h] + 1]), out)
        z = d_ref[q0:q1, 3 * gw:4 * gw].astype(F32)
        o_ref[q0:q1, :] = (out * _silu(z)).astype(o_ref.dtype)


def _fox(dx, gates, qkg, layer):
    b, s, _ = dx.shape
    gw = GROUP_WIDTH
    return _layered_call(
        _fox_kernel,
        grid=(b,),
        in_specs=[_row_spec((None, s, SEG_D)), _row_spec((None, s, SEG_G)), _layer_spec((2, gw))],
        out_specs=_row_spec((None, s, gw)),
        out_shape=jax.ShapeDtypeStruct((b, s, gw), BF16),
        scratch_shapes=[pltpu.VMEM((s, gw), BF16), pltpu.VMEM((s, gw), BF16),
                        pltpu.VMEM((N_HEADS, s, gw), BF16), pltpu.VMEM((SEG_G, s), F32)],
        name="fox",
    )(layer, dx, gates, qkg)


def _mem_kv_kernel(mem_ref, g_ref, w_ref, o_ref):
    mn = _rms_rows(mem_ref[...], g_ref[...]).astype(BF16)
    o_ref[...] = _nn(mn, w_ref[...]).astype(o_ref.dtype)


def _mem_kv(mem, g, wkv):
    b, n_mem, d = mem.shape
    depth = wkv.shape[0]
    return pl.pallas_call(
        _mem_kv_kernel,
        grid=(depth, b),
        in_specs=[
            pl.BlockSpec((None, n_mem, d), lambda l, i: (i, 0, 0)),
            pl.BlockSpec((1, d), lambda l, i: (0, 0)),
            pl.BlockSpec((None, d, 2 * d), lambda l, i: (l, 0, 0)),
        ],
        out_specs=pl.BlockSpec((None, None, n_mem, 2 * d), lambda l, i: (l, i, 0, 0)),
        out_shape=jax.ShapeDtypeStruct((depth, b, n_mem, 2 * d), BF16),
        compiler_params=_params(),
        name="mem_kv",
    )(mem, g, wkv)


def _out_xattn_kernel(x_ref, ya_ref, yb_ref, yc_ref, yd_ref, wout_ref, g_ref, wq_ref, kv_ref, wo_ref,
                      o_ref, *, n_mem_heads):
    d = x_ref.shape[-1]
    dh = d // n_mem_heads
    tm = x_ref.shape[0]
    gain = lambda i: g_ref[i:i + 1, :]
    parts = [pl.ds(j * (tm // XATTN_PARTS), tm // XATTN_PARTS) for j in range(XATTN_PARTS)]
    ys = [jnp.concatenate([ya_ref[r, :], yb_ref[r, :], yc_ref[r, :], yd_ref[r, :]], axis=-1) for r in parts]
    mixed = [_nn(y, wout_ref[...]) for y in ys]
    xs = [x_ref[r, :] + _rms_rows(m, gain(_G_POST)) for r, m in zip(parts, mixed)]
    hms = [_rms_rows(x, gain(_G_XPRE)).astype(BF16) for x in xs]
    qs = [_nn(hm, wq_ref[...]).astype(BF16) for hm in hms]
    outs = [[] for _ in parts]
    for h in range(n_mem_heads):
        kh = kv_ref[:, h * dh:(h + 1) * dh]
        vh = kv_ref[:, d + h * dh:d + (h + 1) * dh]
        logits = [_nt(q[:, h * dh:(h + 1) * dh], kh) * (dh ** -0.5) for q in qs]
        ps = [jnp.exp(l - jnp.max(l, axis=-1, keepdims=True)) for l in logits]
        ps = [(p / jnp.sum(p, axis=-1, keepdims=True)).astype(BF16) for p in ps]
        for j, p in enumerate(ps):
            outs[j].append(_nn(p, vh).astype(BF16))
    attn = [_nn(jnp.concatenate(o, axis=-1), wo_ref[...]) for o in outs]
    for r, x, o in zip(parts, xs, attn):
        o_ref[r, :] = x + _rms_rows(o, gain(_G_XPOST))


def _out_xattn(x2, ys, wout, gains, wq, kv, wo, seq_len, layer, n_mem_heads=4):
    m, d = x2.shape
    gw = GROUP_WIDTH
    tm = min(XATTN_ROW_TILE, seq_len)
    per_seq = seq_len // tm
    n_mem = kv.shape[2]
    return _layered_call(
        functools.partial(_out_xattn_kernel, n_mem_heads=n_mem_heads),
        grid=(m // tm,),
        in_specs=[_row_spec((tm, d))] + [_row_spec((tm, gw))] * 4 + [
            _layer_spec((4 * gw, d)),
            _layer_spec((_G_ROWS, d)),
            _layer_spec((d, d)),
            pl.BlockSpec((None, None, n_mem, 2 * d), lambda i, layer: (layer[0], i // per_seq, 0, 0)),
            _layer_spec((d, d)),
        ],
        out_specs=_row_spec((tm, d)),
        out_shape=jax.ShapeDtypeStruct((m, d), F32),
        name="out_xattn",
    )(layer, x2, *ys, wout, gains, wq, kv, wo)


_IN_SIZES = (("a_q", GROUP_WIDTH), ("a_k", GROUP_WIDTH), ("a_v", GROUP_WIDTH), ("a_i", N_HEADS), ("a_f", N_HEADS),
             ("a_z", GROUP_WIDTH), ("b_x", GROUP_WIDTH), ("b_z", GROUP_WIDTH),
             ("c_r", GROUP_WIDTH), ("c_k", GROUP_WIDTH), ("c_v", GROUP_WIDTH), ("c_w", LORA), ("c_a", LORA),
             ("c_z", GROUP_WIDTH), ("d_q", GROUP_WIDTH), ("d_k", GROUP_WIDTH), ("d_v", GROUP_WIDTH),
             ("d_f", N_HEADS), ("d_z", GROUP_WIDTH))
_IN_ORDER = ("a_q", "a_k", "a_v", "a_z", "b_x", "b_z", "c_r", "c_k", "c_v", "c_z", "c_w", "c_a",
             "d_q", "d_k", "d_v", "d_z", "a_i", "a_f", "d_f")
N_IN = sum(size for _, size in _IN_SIZES)


def _relayout_in(w):
    pieces, off = {}, 0
    for name, size in _IN_SIZES:
        pieces[name] = w[..., off:off + size]
        off += size
    pad = jnp.zeros(w.shape[:-1] + (SEG_G - 3 * N_HEADS,), w.dtype)
    return jnp.concatenate([pieces[n] for n in _IN_ORDER] + [pad], axis=-1)


def _relayout_w_kernel(w_ref, o_ref):
    o_ref[...] = _relayout_in(w_ref[...]).astype(o_ref.dtype)


def _relayout_w(w_in):
    depth, d, n_in = w_in.shape
    assert n_in == N_IN
    tr = RELAYOUT_ROWS
    return pl.pallas_call(
        _relayout_w_kernel,
        grid=(depth, d // tr),
        in_specs=[pl.BlockSpec((None, tr, n_in), lambda l, i: (l, i, 0))],
        out_specs=pl.BlockSpec((None, tr, N_PROJ), lambda l, i: (l, i, 0)),
        out_shape=jax.ShapeDtypeStruct((depth, d, N_PROJ), BF16),
        compiler_params=_params(),
        name="relayout_w",
    )(w_in)


def _block_diag(blocks):
    n, k, _ = blocks.shape
    eye = jnp.eye(n, dtype=blocks.dtype)
    return jnp.einsum("gcd,gh->gchd", blocks, eye).reshape(n * k, n * k)


def kernel(x, mem, pre_norm_g, post_norm_g, w_in, b_in, conv_a, norm_a_g, pool_w, pool_scale, shift_mu_c,
           decay_w0, decay_w2, iclr_a0, iclr_a2, key_k, key_a, bonus_u, gn_c_g, gn_c_b, qk_norm_d, w_out,
           mem_norm_g, xattn_pre_g, xattn_post_g, xattn_wq, xattn_wkv, xattn_wo):
    bsz, seq, d = x.shape
    depth = w_in.shape[0]
    gw = GROUP_WIDTH

    w_in_r = _relayout_w(w_in)
    b_in_r = _relayout_in(b_in).reshape(depth, 1, N_PROJ)
    gains = jnp.stack([pre_norm_g, post_norm_g, xattn_pre_g, xattn_post_g]
                      + [jnp.zeros_like(pre_norm_g)] * (_G_ROWS - 4), axis=1)
    mu = shift_mu_c
    zeros_row = jnp.zeros((depth, gw), F32)
    mu_wa = jnp.concatenate([mu[:, 3 * gw:], jnp.zeros((depth, gw - 2 * LORA), F32)], axis=-1)
    rows = [mu[:, 0:gw], mu[:, gw:2 * gw], mu[:, 2 * gw:3 * gw], decay_w0, iclr_a0, key_k, key_a, bonus_u,
            gn_c_g, gn_c_b, mu_wa] + [zeros_row] * (_P_ROWS - 11)
    rwkv_prm = jnp.stack(rows, axis=1)
    zl = jnp.zeros((depth, LORA, gw), F32)
    w2a2 = jnp.concatenate([jnp.concatenate([decay_w2, zl], axis=-1),
                            jnp.concatenate([zl, iclr_a2], axis=-1)], axis=1).astype(BF16)
    pool_bd = jax.vmap(_block_diag)(pool_w).astype(BF16)
    qkg = jnp.tile(qk_norm_d, (1, 1, N_HEADS))
    norm_a = norm_a_g.reshape(depth, 1, gw)
    pool_sc = pool_scale.reshape(depth, 1, gw)
    w_out_b = w_out.astype(BF16)
    wq_b = xattn_wq.astype(BF16)
    wo_b = xattn_wo.astype(BF16)

    kv_all = _mem_kv(mem, mem_norm_g.reshape(1, d), xattn_wkv.astype(BF16))

    x2 = x.reshape(bsz * seq, d)
    sh = lambda u: u.reshape(bsz, seq, u.shape[-1])
    for l in range(depth):
        layer = jnp.full((1,), l, jnp.int32)
        pa, pb, pc, pd, pg = _in_proj(x2, gains, w_in_r, b_in_r, layer)
        gates = sh(pg)
        y_a = _mlstm(sh(pa), gates, conv_a, norm_a, layer)
        y_b = _pool(sh(pb), pool_bd, pool_sc, layer)
        y_c = _rwkv(sh(pc), rwkv_prm, w2a2, layer)
        y_d = _fox(sh(pd), gates, qkg, layer)
        ys = [u.reshape(bsz * seq, gw) for u in (y_a, y_b, y_c, y_d)]
        x2 = _out_xattn(x2, ys, w_out_b, gains, wq_b, kv_all, wo_b, seq, layer)
    return x2.reshape(bsz, seq, d)
```

```python
import functools

import jax
import jax.numpy as jnp
from jax import lax
from jax.experimental import pallas as pl
from jax.experimental.pallas import tpu as pltpu

F32 = jnp.float32
BF16 = jnp.bfloat16

HEAD_DIM = 64
N_HEADS = 4
GROUP_WIDTH = HEAD_DIM * N_HEADS
LANES = 128
SUBLANES = 8
BF16_ROWS = 16
NORM_EPS = 1e-6
GN_EPS = 64e-5
LOG2_E = 1.4426950408889634
POOL_WINDOWS = (2, 4, 8, 16)
CONV_WIDTH_A = 4
LORA = 64
MLSTM_CHUNK = 128
MLSTM_GROUP = 4
RWKV_CHUNK = 64
RWKV_SUB = 16
RWKV_GROUP = 8
RWKV_PAIR = 2
DECAY_SCALE = 0.6065306597126334
FOX_QBLOCK = 256
ROW_TILE = 512
XATTN_ROW_TILE = 1024
XATTN_PARTS = 2
RELAYOUT_COLS = 256
VMEM_LIMIT = 56 * 1024 * 1024

SEG_A = 4 * GROUP_WIDTH
SEG_B = 2 * GROUP_WIDTH
SEG_C = 4 * GROUP_WIDTH + 2 * LORA
SEG_D = 4 * GROUP_WIDTH
SEG_G = LANES
N_PROJ = SEG_A + SEG_B + SEG_C + SEG_D + SEG_G
GATE_I, GATE_F, GATE_D = 0, N_HEADS, 2 * N_HEADS

(_G_PRE, _G_POST, _G_XPRE, _G_XPOST) = range(4)
_G_ROWS = 8

(_P_MU_R, _P_MU_K, _P_MU_V, _P_W0, _P_A0, _P_KEY_K, _P_KEY_A, _P_BONUS, _P_GN_G, _P_GN_B, _P_MU_WA) = range(11)
_P_ROWS = 16


def _lane(shape):
    return lax.broadcasted_iota(jnp.int32, shape, len(shape) - 1)


def _row(shape):
    return lax.broadcasted_iota(jnp.int32, shape, len(shape) - 2)


def _nt(a, b):
    return lax.dot_general(a, b, (((1,), (1,)), ((), ())), preferred_element_type=F32)


def _tn(a, b):
    return lax.dot_general(a, b, (((0,), (0,)), ((), ())), preferred_element_type=F32)


def _nn(a, b):
    return jnp.dot(a, b, preferred_element_type=F32)


def _sigmoid(x):
    return 1.0 / (1.0 + jnp.exp(-x))


def _silu(x):
    return x * _sigmoid(x)


def _softplus(x):
    return jnp.maximum(x, 0.0) + jnp.log(1.0 + jnp.exp(-jnp.abs(x)))


def _log_sigmoid(x):
    return -_softplus(-x)


def _shift_rows(x, s):
    rolled = pltpu.roll(x, shift=s, axis=0)
    return jnp.where(_row(x.shape) >= s, rolled, 0.0)


def _cumsum_rows(x, seg):
    pos = _row(x.shape) & (seg - 1)
    s = 1
    while s < seg:
        x = x + jnp.where(pos >= s, pltpu.roll(x, shift=s, axis=0), 0.0)
        s *= 2
    return x


def _cummax_rows(x, seg):
    pos = _row(x.shape) & (seg - 1)
    s = 1
    while s < seg:
        x = jnp.maximum(x, jnp.where(pos >= s, pltpu.roll(x, shift=s, axis=0), -jnp.inf))
        s *= 2
    return x


def _stack_heads(x):
    hid = _lane(x.shape) >> 6
    zero = jnp.zeros_like(x)
    return jnp.concatenate([jnp.where(hid == h, x, zero) for h in range(N_HEADS)], axis=0)


def _per_head(cols, shape):
    hid = _lane(shape) >> 6
    out = jnp.broadcast_to(cols[N_HEADS - 1], shape)
    for h in range(N_HEADS - 2, -1, -1):
        out = jnp.where(hid == h, cols[h], out)
    return out


def _per_head_rows(vals, shape):
    hid = _row(shape) >> 6
    out = jnp.broadcast_to(vals[N_HEADS - 1], shape)
    for h in range(N_HEADS - 2, -1, -1):
        out = jnp.where(hid == h, vals[h], out)
    return out


def _head_cols(x, first):
    return [x[:, first + h:first + h + 1] for h in range(N_HEADS)]


def _block_ones(n):
    return jnp.where((_row((n, n)) >> 6) == (_lane((n, n)) >> 6), 1.0, 0.0).astype(BF16)


def _group_sum(x, ones_bd):
    hi = x.astype(BF16)
    lo = (x - hi.astype(F32)).astype(BF16)
    return _nn(hi, ones_bd) + _nn(lo, ones_bd)


def _rms_rows(x, g):
    return x * lax.rsqrt(jnp.mean(x * x, axis=-1, keepdims=True) + NORM_EPS) * g


def _params(**kw):
    return pltpu.CompilerParams(vmem_limit_bytes=VMEM_LIMIT, **kw)


def _layer_spec(shape):
    zeros = (0,) * len(shape)
    return pl.BlockSpec((None,) + tuple(shape), lambda i, layer: (layer[0],) + zeros)


def _row_spec(shape):
    zeros = (0,) * (len(shape) - 1)
    return pl.BlockSpec(tuple(shape), lambda i, layer: (i,) + zeros)


def _layered_call(body, grid, in_specs, out_specs, out_shape, scratch_shapes=(), name=None):
    def with_layer(layer_ref, *refs):
        del layer_ref
        body(*refs)

    return pl.pallas_call(
        with_layer,
        grid_spec=pltpu.PrefetchScalarGridSpec(num_scalar_prefetch=1, grid=grid, in_specs=in_specs,
                                               out_specs=out_specs, scratch_shapes=scratch_shapes),
        out_shape=out_shape, compiler_params=_params(), name=name)


def _in_proj_kernel(x_ref, g_ref, w_ref, b_ref, oa_ref, ob_ref, oc_ref, od_ref, og_ref):
    h = _rms_rows(x_ref[...], g_ref[_G_PRE:_G_PRE + 1, :]).astype(BF16)
    off = 0
    for o_ref, width in ((oa_ref, SEG_A), (ob_ref, SEG_B), (oc_ref, SEG_C), (od_ref, SEG_D), (og_ref, SEG_G)):
        acc = _nt(h, w_ref[off:off + width, :]) + b_ref[:, off:off + width]
        o_ref[...] = acc.astype(o_ref.dtype)
        off += width


def _in_proj(x2, gains, w, b, layer):
    m, d = x2.shape
    tm = min(ROW_TILE, m)
    widths = (SEG_A, SEG_B, SEG_C, SEG_D, SEG_G)
    dtypes = (BF16, BF16, BF16, BF16, F32)
    return _layered_call(
        _in_proj_kernel,
        grid=(m // tm,),
        in_specs=[_row_spec((tm, d)), _layer_spec((_G_ROWS, d)), _layer_spec((N_PROJ, d)), _layer_spec((1, N_PROJ))],
        out_specs=[_row_spec((tm, wd)) for wd in widths],
        out_shape=[jax.ShapeDtypeStruct((m, wd), dt) for wd, dt in zip(widths, dtypes)],
        name="in_proj",
    )(layer, x2, gains, w, b)


def _mlstm_kernel(a_ref, g_ref, conv_ref, ng_ref, o_ref,
                  q_s, k_s, bc_s, gt_s, bt_s, cl_s, nl_s, ml_s):
    s_len = a_ref.shape[0]
    L = MLSTM_CHUNK
    nc = s_len // L
    grp = MLSTM_GROUP
    gw = GROUP_WIDTH

    rows_of = lambda c: pl.ds(pl.multiple_of(c * L, L), L)

    def conv_and_gates(c, carry):
        r0 = pl.multiple_of(c * L, L)
        halo = a_ref[pl.ds(pl.multiple_of(jnp.maximum(r0 - BF16_ROWS, 0), BF16_ROWS), BF16_ROWS), :2 * gw]
        halo = halo.astype(F32) * jnp.where(r0 > 0, 1.0, 0.0)
        qk = jnp.concatenate([halo, a_ref[rows_of(c), :2 * gw].astype(F32)], axis=0)
        acc = qk * conv_ref[CONV_WIDTH_A - 1:CONV_WIDTH_A, :]
        for j in range(1, CONV_WIDTH_A):
            acc = acc + pltpu.roll(qk, shift=j, axis=0) * conv_ref[CONV_WIDTH_A - 1 - j:CONV_WIDTH_A - j, :]
        acc = _silu(acc[BF16_ROWS:, :])
        q_s[rows_of(c), :] = acc[:, :gw].astype(BF16)
        k_s[rows_of(c), :] = (acc[:, gw:] * (HEAD_DIM ** -0.5)).astype(BF16)
        gates = g_ref[rows_of(c), :]
        bcum = _cumsum_rows(_log_sigmoid(gates), L)
        bc_s[rows_of(c), :] = bcum
        gt_s[c] = jnp.transpose(gates)
        bt_s[c] = jnp.transpose(bcum)
        return carry

    ones_bd = _block_ones(gw)
    ones_rows = jnp.where((_row((N_HEADS * L, gw)) // L) == (_lane((N_HEADS * L, gw)) >> 6), 1.0, 0.0).astype(BF16)
    tril = _lane((L, L)) <= _row((L, L))
    bd_mask = (_row((gw, gw)) >> 6) == (_lane((gw, gw)) >> 6)
    rows8 = lambda u: jnp.broadcast_to(u, (SUBLANES, u.shape[-1]))

    def local_state(i, carry):
        cs = [i * grp + j for j in range(grp)]
        for c in cs:
            conv_and_gates(c, carry)
        ks = [k_s[rows_of(c), :] for c in cs]
        e_fulls, ves = [], []
        for c in cs:
            v = a_ref[rows_of(c), 2 * gw:3 * gw]
            ig = pltpu.roll(g_ref[rows_of(c), :], shift=GATE_F - GATE_I, axis=1)
            w_loc = bc_s[pl.ds(pl.multiple_of(c * L, L) + L - 1, 1), :] - bc_s[rows_of(c), :] + ig
            m_loc = jnp.max(w_loc, axis=0, keepdims=True)
            ml_s[c] = rows8(m_loc)
            e_full = _per_head(_head_cols(jnp.exp(w_loc - m_loc), GATE_F), (L, gw))
            e_fulls.append(e_full)
            ves.append((v.astype(F32) * e_full).astype(BF16))
        c_locs = [_tn(ve, k) for ve, k in zip(ves, ks)]
        for c, c_loc, e_full, k in zip(cs, c_locs, e_fulls, ks):
            cl_s[c] = jnp.where(bd_mask, c_loc, 0.0)
            nl_s[c] = rows8(jnp.sum(e_full * k.astype(F32), axis=0, keepdims=True))
        return carry

    lax.fori_loop(0, nc // grp, local_state, 0)

    def scan(c, carry):
        c_prev, n_prev, m_prev = carry
        g_tot = bc_s[pl.ds(pl.multiple_of(c * L, L) + L - 1, 1), :]
        m_loc = ml_s[c][0:1, :]
        m_new = jnp.maximum(g_tot + m_prev, m_loc)
        a_row = jnp.exp(g_tot + m_prev - m_new)
        b_row = jnp.exp(m_loc - m_new)
        a_h, b_h = _head_cols(a_row, GATE_F), _head_cols(b_row, GATE_F)
        c_new = _per_head_rows(a_h, (gw, gw)) * c_prev + _per_head_rows(b_h, (gw, gw)) * cl_s[c]
        n_new = _per_head(a_h, (1, gw)) * n_prev + _per_head(b_h, (1, gw)) * nl_s[c][0:1, :]
        return c_new, n_new, m_new

    def outputs(i, carry):
        cs = [i * grp + j for j in range(grp)]
        states = [carry]
        for c in cs:
            states.append(scan(c, states[-1]))
        hid = _lane((L, gw)) >> 6
        qs = [q_s[rows_of(c), :] for c in cs]
        ks = [k_s[rows_of(c), :] for c in cs]
        vs = [a_ref[rows_of(c), 2 * gw:3 * gw] for c in cs]
        logits = [[_nt(jnp.where(hid == h, q, jnp.zeros_like(q)), k) for h in range(N_HEADS)] for q, k in zip(qs, ks)]
        num_inter = [_nt(q, st[0].astype(BF16)) for q, st in zip(qs, states)]
        den_inter = [_group_sum(q.astype(F32) * st[1], ones_bd) for q, st in zip(qs, states)]
        shape = (L, gw)
        s_cat, iws, floors = [], [], []
        for j, c in enumerate(cs):
            b_col = bc_s[rows_of(c), :]
            g_row = gt_s[c]
            b_row = bt_s[c]
            ig = pltpu.roll(g_ref[rows_of(c), :], shift=GATE_F - GATE_I, axis=1)
            m_inter = b_col + states[j][2]
            m_t = jnp.maximum(m_inter, b_col + _cummax_rows(ig - b_col, L))
            iws.append(_per_head(_head_cols(jnp.exp(m_inter - m_t), GATE_F), shape))
            floors.append(_per_head(_head_cols(jnp.exp(-m_t), GATE_F), shape))
            col_arg = b_col - m_t
            s_list = []
            for h in range(N_HEADS):
                row_arg = g_row[GATE_I + h:GATE_I + h + 1, :] - b_row[GATE_F + h:GATE_F + h + 1, :]
                arg = jnp.where(tril, col_arg[:, GATE_F + h:GATE_F + h + 1] + row_arg, -jnp.inf)
                s_list.append((logits[j][h] * jnp.exp(arg)).astype(BF16))
            s_cat.append(jnp.concatenate(s_list, axis=1))
        num_intra = [_nn(s, _stack_heads(v)) for s, v in zip(s_cat, vs)]
        den_intra = [_nn(s, ones_rows) for s in s_cat]
        hhs = []
        for j in range(grp):
            num = num_intra[j] + iws[j] * num_inter[j]
            den = den_intra[j] + iws[j] * den_inter[j]
            hhs.append(num / jnp.maximum(jnp.abs(den), floors[j]))
        mss = [_group_sum(hh * hh, ones_bd) for hh in hhs]
        for c, hh, ms in zip(cs, hhs, mss):
            z = a_ref[rows_of(c), 3 * gw:4 * gw].astype(F32)
            y = hh * lax.rsqrt(ms * (1.0 / HEAD_DIM) + NORM_EPS) * ng_ref[...] * _silu(z)
            o_ref[rows_of(c), :] = y.astype(o_ref.dtype)
        return states[-1]

    lax.fori_loop(0, nc // grp, outputs,
                  (jnp.zeros((gw, gw), F32), jnp.zeros((1, gw), F32), jnp.zeros((1, LANES), F32)))


def _mlstm(a, gates, conv, ng, layer):
    b, s, _ = a.shape
    gw = GROUP_WIDTH
    L = MLSTM_CHUNK
    nc = s // L
    assert nc % MLSTM_GROUP == 0
    return _layered_call(
        _mlstm_kernel,
        grid=(b,),
        in_specs=[_row_spec((None, s, SEG_A)), _row_spec((None, s, SEG_G)),
                  _layer_spec((CONV_WIDTH_A, 2 * gw)), _layer_spec((1, gw))],
        out_specs=_row_spec((None, s, gw)),
        out_shape=jax.ShapeDtypeStruct((b, s, gw), BF16),
        scratch_shapes=[
            pltpu.VMEM((s, gw), BF16), pltpu.VMEM((s, gw), BF16),
            pltpu.VMEM((s, SEG_G), F32),
            pltpu.VMEM((nc, SEG_G, L), F32), pltpu.VMEM((nc, SEG_G, L), F32),
            pltpu.VMEM((nc, gw, gw), F32), pltpu.VMEM((nc, SUBLANES, gw), F32), pltpu.VMEM((nc, SUBLANES, LANES), F32),
        ],
        name="mlstm",
    )(layer, a, gates, conv, ng)


def _pool_kernel(b_ref, w_ref, sc_ref, o_ref):
    gw = GROUP_WIDTH
    u = b_ref[:, :gw].astype(F32)
    z = b_ref[:, gw:].astype(F32)
    pos = _row(u.shape).astype(F32)
    grp = _lane(u.shape) >> 6
    run, width = u, 1
    window = jnp.full(u.shape, float(POOL_WINDOWS[-1]), F32)
    sums = []
    for win in POOL_WINDOWS:
        while width < win:
            run = run + _shift_rows(run, width)
            width *= 2
        sums.append(run)
    window_sum = sums[-1]
    for gi in range(len(POOL_WINDOWS) - 2, -1, -1):
        window_sum = jnp.where(grp == gi, sums[gi], window_sum)
        window = jnp.where(grp == gi, float(POOL_WINDOWS[gi]), window)
    pooled = window_sum / jnp.minimum(pos + 1.0, window) - u
    mixed = _nn(pooled.astype(BF16), w_ref[...])
    o_ref[...] = (mixed * sc_ref[...] * _silu(z)).astype(o_ref.dtype)


def _pool(bx, w_bd, scale, layer):
    b, s, _ = bx.shape
    gw = GROUP_WIDTH
    return _layered_call(
        _pool_kernel,
        grid=(b,),
        in_specs=[_row_spec((None, s, SEG_B)), _layer_spec((gw, gw)), _layer_spec((1, gw))],
        out_specs=_row_spec((None, s, gw)),
        out_shape=jax.ShapeDtypeStruct((b, s, gw), BF16),
        name="pool",
    )(layer, bx, w_bd, scale)


def _cat_mm(xs, ys):
    return [_nn(x.astype(BF16), _stack_heads(y.astype(BF16))) for x, y in zip(xs, ys)]


def _unit_lower_inverse(mats):
    t = RWKV_CHUNK
    shift = RWKV_SUB.bit_length() - 1
    shape = mats[0].shape
    rb = _row(shape) >> shift
    col = _lane(shape) & (t - 1)
    cb = col >> shift
    eye = jnp.where(col == _row(shape), 1.0, 0.0)
    add = lambda xs, ys: [x + y for x, y in zip(xs, ys)]
    d = [jnp.where(rb == cb, a, 0.0) for a in mats]
    e = [jnp.where(rb == cb, 0.0, a) for a in mats]
    d2 = _cat_mm(d, d)
    d4 = _cat_mm(d2, d2)
    x = [eye + di for di in d]
    x = add(x, _cat_mm(x, d2))
    d8 = _cat_mm(d4, d4)
    x = add(x, _cat_mm(x, d4))
    dinv = add(x, _cat_mm(x, d8))
    f = _cat_mm(dinv, e)
    f2 = _cat_mm(f, f)
    w = [eye + fi for fi in f]
    w = add(w, _cat_mm(w, f2))
    return _cat_mm(w, dinv)


def _rwkv_kernel(c_ref, p_ref, w2a2_ref, o_ref,
                 v_s, bonus_s, pr_s, qk_s, ul_s, ar_s, ge_s, ub_s, rs_s):
    n_seq, s_len = c_ref.shape[0], c_ref.shape[1]
    gw = GROUP_WIDTH
    T = RWKV_CHUNK
    grp = RWKV_GROUP
    rows = T * grp
    nc = s_len // T
    prm = lambda i: p_ref[i:i + 1, :]

    ones_bd = _block_ones(gw)
    col = _lane((T, N_HEADS * T)) & (T - 1)
    row = _row((T, N_HEADS * T))
    strict = col < row
    incl = col <= row
    bd_mask = (_row((gw, gw)) >> 6) == (_lane((gw, gw)) >> 6)
    first_row = _row((rows, gw)) == 0

    def prepare(e, i, carry):
        r0 = pl.multiple_of(i * rows, rows)
        tile = pl.ds(r0, rows)
        halo = c_ref[e, pl.ds(pl.multiple_of(jnp.maximum(r0 - BF16_ROWS, 0), BF16_ROWS), BF16_ROWS), :]
        prev = halo[BF16_ROWS - 1:BF16_ROWS, :].astype(F32) * jnp.where(r0 > 0, 1.0, 0.0)

        def shifted(lo, hi, mu):
            u = c_ref[e, tile, lo:hi].astype(F32)
            before = jnp.where(first_row[:, :hi - lo], prev[:, lo:hi], pltpu.roll(u, shift=1, axis=0))
            return u + mu * (before - u)

        wa = shifted(4 * gw, 4 * gw + 2 * LORA, p_ref[_P_MU_WA:_P_MU_WA + 1, :2 * LORA])
        wa = jnp.where(_lane(wa.shape) < LORA, jnp.tanh(wa), wa)
        lora = _nn(wa.astype(BF16), w2a2_ref[...])
        ld = -DECAY_SCALE * _sigmoid(prm(_P_W0) + lora[:, :gw])
        a = _sigmoid(prm(_P_A0) + lora[:, gw:])
        r = shifted(0, gw, prm(_P_MU_R))
        k = shifted(gw, 2 * gw, prm(_P_MU_K))
        v = shifted(2 * gw, 3 * gw, prm(_P_MU_V))
        kk = k * prm(_P_KEY_K)
        kk = kk / jnp.maximum(jnp.sqrt(_group_sum(kk * kk, ones_bd)), 1e-12)
        k = k * (1.0 + (a - 1.0) * prm(_P_KEY_A))
        bonus_s[e, tile, :] = (_group_sum(r * k * prm(_P_BONUS), ones_bd) * v).astype(BF16)
        vb = v.astype(BF16)
        v_s[e, tile, :] = vb

        lc = _cumsum_rows(ld, T)
        g_inc = jnp.exp(lc)
        g_inv = jnp.exp(-lc)
        pt = (-kk * jnp.exp(lc - ld)).astype(BF16)
        rt = (r * g_inc).astype(BF16)
        qt = (kk * a * g_inv).astype(BF16)
        kt = (k * g_inv).astype(BF16)

        part = lambda u, j: u[j * T:(j + 1) * T, :]
        chunks = range(grp)
        prs = [jnp.concatenate([part(pt, j), part(rt, j)], axis=0) for j in chunks]
        gs = [_nt(prs[j], jnp.concatenate([_stack_heads(part(qt, j)), _stack_heads(part(kt, j))], axis=0))
              for j in chunks]
        a_pq = [jnp.where(strict, g[:T, :N_HEADS * T], 0.0) for g in gs]
        a_pk = [jnp.where(strict, g[:T, N_HEADS * T:], 0.0).astype(BF16) for g in gs]
        apv = [_nn(a_pk[j], _stack_heads(part(vb, j))) for j in chunks]
        tinv = [t.astype(BF16) for t in _unit_lower_inverse(a_pq)]
        p_fold = [_nn(tinv[j], _stack_heads(part(pt, j))) for j in chunks]
        u_loc = [_nn(tinv[j], _stack_heads(apv[j].astype(BF16))) for j in chunks]
        for j in chunks:
            c = i * grp + j
            pr_s[e, c] = jnp.concatenate([p_fold[j].astype(BF16), part(rt, j)], axis=0)
            qk_s[e, c] = jnp.concatenate([part(qt, j), part(kt, j)], axis=0)
            ul_s[e, c] = u_loc[j].astype(BF16)
            ar_s[e, c] = jnp.concatenate([jnp.where(incl, gs[j][T:, :N_HEADS * T], 0.0),
                                          jnp.where(incl, gs[j][T:, N_HEADS * T:], 0.0)], axis=1).astype(BF16)
            ge_s[e, c] = jnp.broadcast_to(part(g_inc, j)[T - 1:T, :], (SUBLANES, gw))
        return carry

    for e in range(n_seq):
        lax.fori_loop(0, nc // grp, functools.partial(prepare, e), 0)

    seqs = range(n_seq)

    rows_of = lambda c: pl.ds(pl.multiple_of(c * T, T), T)

    def project(c, states):
        return [_nt(pr_s[e, c], states[e].astype(BF16)) for e in seqs]

    def correct(c, prs):
        ubs = [(prs[e][:T] + ul_s[e, c].astype(F32)).astype(BF16) for e in seqs]
        dss = [_tn(jnp.concatenate([ubs[e], v_s[e, rows_of(c), :]], axis=0), qk_s[e, c]) for e in seqs]
        for e in seqs:
            ub_s[e, c] = ubs[e]
            rs_s[e, c] = prs[e][T:].astype(BF16)
        return dss

    def advance(c, states, dss):
        return tuple((states[e] + jnp.where(bd_mask, dss[e], 0.0)) * ge_s[e, c][0:1, :] for e in seqs)

    def readout(c):
        return [rs_s[e, c].astype(F32)
                + _nn(ar_s[e, c], jnp.concatenate([_stack_heads(ub_s[e, c]), _stack_heads(v_s[e, rows_of(c), :])], axis=0))
                for e in seqs]

    def centre(ys):
        means = [_group_sum(y, ones_bd) * (1.0 / HEAD_DIM) for y in ys]
        return [y - m for y, m in zip(ys, means)]

    def finish(c, ycs):
        vrs = [_group_sum(yc * yc, ones_bd) * (1.0 / HEAD_DIM) for yc in ycs]
        for e in seqs:
            y = ycs[e] * lax.rsqrt(vrs[e] + GN_EPS) * prm(_P_GN_G) + prm(_P_GN_B) + bonus_s[e, rows_of(c), :].astype(F32)
            z = c_ref[e, rows_of(c), 3 * gw:4 * gw].astype(F32)
            o_ref[e, rows_of(c), :] = (y * _silu(z)).astype(o_ref.dtype)

    def step(c, states):
        prs = project(c, states)
        ys = readout(c - 1)
        dss = correct(c, prs)
        ycs = centre(ys)
        new_states = advance(c, states, dss)
        finish(c - 1, ycs)
        return new_states

    zero_states = tuple(jnp.zeros((gw, gw), F32) for _ in seqs)
    states = advance(0, zero_states, correct(0, project(0, zero_states)))
    lax.fori_loop(1, nc, step, states, unroll=2)
    finish(nc - 1, centre(readout(nc - 1)))


def _rwkv(cx, prm, w2a2, layer):
    b, s, _ = cx.shape
    gw = GROUP_WIDTH
    T = RWKV_CHUNK
    nc = s // T
    n_seq = RWKV_PAIR if b % RWKV_PAIR == 0 else 1
    assert nc % RWKV_GROUP == 0
    per_chunk = lambda r, w, dt: pltpu.VMEM((n_seq, nc, r, w), dt)
    return _layered_call(
        _rwkv_kernel,
        grid=(b // n_seq,),
        in_specs=[_row_spec((n_seq, s, SEG_C)), _layer_spec((_P_ROWS, gw)), _layer_spec((2 * LORA, 2 * gw))],
        out_specs=_row_spec((n_seq, s, gw)),
        out_shape=jax.ShapeDtypeStruct((b, s, gw), BF16),
        scratch_shapes=[
            pltpu.VMEM((n_seq, s, gw), BF16), pltpu.VMEM((n_seq, s, gw), BF16),
            per_chunk(2 * T, gw, BF16), per_chunk(2 * T, gw, BF16),
            per_chunk(T, gw, BF16),
            per_chunk(T, 2 * N_HEADS * T, BF16), per_chunk(SUBLANES, gw, F32),
            per_chunk(T, gw, BF16), per_chunk(T, gw, BF16),
        ],
        name="rwkv",
    )(layer, cx, prm, w2a2)


def _fox_kernel(d_ref, g_ref, qkg_ref, o_ref, q_s, k_s, vh_s, ft_s):
    s_len = d_ref.shape[0]
    gw = GROUP_WIDTH
    tq = min(FOX_QBLOCK, s_len)
    ones_bd = _block_ones(gw)

    def head_norm(u, g):
        ms = _group_sum(u * u, ones_bd) * (1.0 / HEAD_DIM)
        return u * lax.rsqrt(ms + NORM_EPS) * g

    q_s[...] = (head_norm(d_ref[:, 0:gw].astype(F32), qkg_ref[0:1, :]) * (LOG2_E * HEAD_DIM ** -0.5)).astype(BF16)
    k_s[...] = head_norm(d_ref[:, gw:2 * gw].astype(F32), qkg_ref[1:2, :]).astype(BF16)
    v = d_ref[:, 2 * gw:3 * gw]
    den_lane = [((h + 1) % N_HEADS) * HEAD_DIM for h in range(N_HEADS)]
    for h in range(N_HEADS):
        spare = jnp.where(_lane(v.shape) == den_lane[h], 1.0, 0.0).astype(BF16)
        vh_s[h] = jnp.where((_lane(v.shape) >> 6) == h, v, spare)
    ft_s[...] = jnp.transpose(_cumsum_rows(_log_sigmoid(g_ref[...]), s_len) * LOG2_E)

    tril = _lane((tq, tq)) <= _row((tq, tq))
    hid = _lane((tq, gw)) >> 6
    for blk in range(s_len // tq):
        q0, q1 = blk * tq, (blk + 1) * tq
        q = q_s[q0:q1, :]
        out = jnp.zeros((tq, gw), F32)
        for h in range(N_HEADS):
            qh = jnp.where(hid == h, q, jnp.zeros_like(q))
            f_row = ft_s[GATE_D + h:GATE_D + h + 1, :]
            l_diag = jnp.where(tril, _nt(qh, k_s[q0:q1, :]) - f_row[:, q0:q1], -jnp.inf)
            mx = jnp.max(l_diag, axis=-1, keepdims=True)
            if blk:
                l_off = _nt(qh, k_s[0:q0, :]) - f_row[:, 0:q0]
                mx = jnp.maximum(mx, jnp.max(l_off, axis=-1, keepdims=True))
            acc = _nn(jnp.exp2(l_diag - mx).astype(BF16), vh_s[h, q0:q1, :])
            if blk:
                acc = acc + _nn(jnp.exp2(l_off - mx).astype(BF16), vh_s[h, 0:q0, :])
            out = jnp.where(hid == h, acc * (1.0 / acc[:, den_lane[h]:den_lane[h] + 1]), out)
        z = d_ref[q0:q1, 3 * gw:4 * gw].astype(F32)
        o_ref[q0:q1, :] = (out * _silu(z)).astype(o_ref.dtype)


def _fox(dx, gates, qkg, layer):
    b, s, _ = dx.shape
    gw = GROUP_WIDTH
    return _layered_call(
        _fox_kernel,
        grid=(b,),
        in_specs=[_row_spec((None, s, SEG_D)), _row_spec((None, s, SEG_G)), _layer_spec((2, gw))],
        out_specs=_row_spec((None, s, gw)),
        out_shape=jax.ShapeDtypeStruct((b, s, gw), BF16),
        scratch_shapes=[pltpu.VMEM((s, gw), BF16), pltpu.VMEM((s, gw), BF16),
                        pltpu.VMEM((N_HEADS, s, gw), BF16), pltpu.VMEM((SEG_G, s), F32)],
        name="fox",
    )(layer, dx, gates, qkg)


def _mem_kv_kernel(mem_ref, g_ref, w_ref, o_ref):
    mn = _rms_rows(mem_ref[...], g_ref[...]).astype(BF16)
    o_ref[...] = _nn(mn, w_ref[...]).astype(o_ref.dtype)


def _mem_kv(mem, g, wkv):
    b, n_mem, d = mem.shape
    depth = wkv.shape[0]
    return pl.pallas_call(
        _mem_kv_kernel,
        grid=(depth, b),
        in_specs=[
            pl.BlockSpec((None, n_mem, d), lambda l, i: (i, 0, 0)),
            pl.BlockSpec((1, d), lambda l, i: (0, 0)),
            pl.BlockSpec((None, d, 2 * d), lambda l, i: (l, 0, 0)),
        ],
        out_specs=pl.BlockSpec((None, None, n_mem, 2 * d), lambda l, i: (l, i, 0, 0)),
        out_shape=jax.ShapeDtypeStruct((depth, b, n_mem, 2 * d), BF16),
        compiler_params=_params(),
        name="mem_kv",
    )(mem, g, wkv)


def _out_xattn_kernel(x_ref, ya_ref, yb_ref, yc_ref, yd_ref, wout_ref, g_ref, wq_ref, kv_ref, wo_ref,
                      o_ref, *, n_mem_heads):
    d = x_ref.shape[-1]
    dh = d // n_mem_heads
    tm = x_ref.shape[0]
    gain = lambda i: g_ref[i:i + 1, :]
    parts = [pl.ds(j * (tm // XATTN_PARTS), tm // XATTN_PARTS) for j in range(XATTN_PARTS)]
    ys = [jnp.concatenate([ya_ref[r, :], yb_ref[r, :], yc_ref[r, :], yd_ref[r, :]], axis=-1) for r in parts]
    mixed = [_nn(y, wout_ref[...]) for y in ys]
    xs = [x_ref[r, :] + _rms_rows(m, gain(_G_POST)) for r, m in zip(parts, mixed)]
    hms = [_rms_rows(x, gain(_G_XPRE)).astype(BF16) for x in xs]
    qs = [_nn(hm, wq_ref[...]).astype(BF16) for hm in hms]
    outs = [[] for _ in parts]
    for h in range(n_mem_heads):
        kh = kv_ref[:, h * dh:(h + 1) * dh]
        vh = kv_ref[:, d + h * dh:d + (h + 1) * dh]
        logits = [_nt(q[:, h * dh:(h + 1) * dh], kh) * (dh ** -0.5) for q in qs]
        ps = [jnp.exp(l - jnp.max(l, axis=-1, keepdims=True)) for l in logits]
        ps = [(p / jnp.sum(p, axis=-1, keepdims=True)).astype(BF16) for p in ps]
        for j, p in enumerate(ps):
            outs[j].append(_nn(p, vh).astype(BF16))
    attn = [_nn(jnp.concatenate(o, axis=-1), wo_ref[...]) for o in outs]
    for r, x, o in zip(parts, xs, attn):
        o_ref[r, :] = x + _rms_rows(o, gain(_G_XPOST))


def _out_xattn(x2, ys, wout, gains, wq, kv, wo, seq_len, layer, n_mem_heads=4):
    m, d = x2.shape
    gw = GROUP_WIDTH
    tm = min(XATTN_ROW_TILE, seq_len)
    per_seq = seq_len // tm
    n_mem = kv.shape[2]
    return _layered_call(
        functools.partial(_out_xattn_kernel, n_mem_heads=n_mem_heads),
        grid=(m // tm,),
        in_specs=[_row_spec((tm, d))] + [_row_spec((tm, gw))] * 4 + [
            _layer_spec((4 * gw, d)),
            _layer_spec((_G_ROWS, d)),
            _layer_spec((d, d)),
            pl.BlockSpec((None, None, n_mem, 2 * d), lambda i, layer: (layer[0], i // per_seq, 0, 0)),
            _layer_spec((d, d)),
        ],
        out_specs=_row_spec((tm, d)),
        out_shape=jax.ShapeDtypeStruct((m, d), F32),
        name="out_xattn",
    )(layer, x2, *ys, wout, gains, wq, kv, wo)


_IN_SIZES = (("a_q", GROUP_WIDTH), ("a_k", GROUP_WIDTH), ("a_v", GROUP_WIDTH), ("a_i", N_HEADS), ("a_f", N_HEADS),
             ("a_z", GROUP_WIDTH), ("b_x", GROUP_WIDTH), ("b_z", GROUP_WIDTH),
             ("c_r", GROUP_WIDTH), ("c_k", GROUP_WIDTH), ("c_v", GROUP_WIDTH), ("c_w", LORA), ("c_a", LORA),
             ("c_z", GROUP_WIDTH), ("d_q", GROUP_WIDTH), ("d_k", GROUP_WIDTH), ("d_v", GROUP_WIDTH),
             ("d_f", N_HEADS), ("d_z", GROUP_WIDTH))
_IN_ORDER = ("a_q", "a_k", "a_v", "a_z", "b_x", "b_z", "c_r", "c_k", "c_v", "c_z", "c_w", "c_a",
             "d_q", "d_k", "d_v", "d_z", "a_i", "a_f", "d_f")
N_IN = sum(size for _, size in _IN_SIZES)


def _relayout_in(w, axis):
    pieces, off = {}, 0
    for name, size in _IN_SIZES:
        pieces[name] = lax.slice_in_dim(w, off, off + size, axis=axis)
        off += size
    pad_shape = list(w.shape)
    pad_shape[axis] = SEG_G - 3 * N_HEADS
    return jnp.concatenate([pieces[n] for n in _IN_ORDER] + [jnp.zeros(pad_shape, w.dtype)], axis=axis)


def _relayout_w_kernel(w_ref, o_ref):
    src = {}
    off = 0
    for name, size in _IN_SIZES:
        src[name] = (off, size)
        off += size
    dst = 0
    gate_rows = []
    for name in _IN_ORDER:
        start, size = src[name]
        if size < BF16_ROWS:
            gate_rows.append(w_ref[start:start + size, :])
        else:
            o_ref[dst:dst + size, :] = w_ref[start:start + size, :].astype(o_ref.dtype)
            dst += size
    used = sum(g.shape[0] for g in gate_rows)
    gate_rows.append(jnp.zeros((o_ref.shape[0] - dst - used, o_ref.shape[1]), F32))
    o_ref[dst:, :] = jnp.concatenate(gate_rows, axis=0).astype(o_ref.dtype)


def _relayout_w(w_t):
    depth, n_in, d = w_t.shape
    assert n_in == N_IN
    tc = RELAYOUT_COLS
    return pl.pallas_call(
        _relayout_w_kernel,
        grid=(depth, d // tc),
        in_specs=[pl.BlockSpec((None, n_in, tc), lambda l, i: (l, 0, i))],
        out_specs=pl.BlockSpec((None, N_PROJ, tc), lambda l, i: (l, 0, i)),
        out_shape=jax.ShapeDtypeStruct((depth, N_PROJ, d), BF16),
        compiler_params=_params(),
        name="relayout_w",
    )(w_t)


def _block_diag(blocks):
    n, k, _ = blocks.shape
    eye = jnp.eye(n, dtype=blocks.dtype)
    return jnp.einsum("gcd,gh->gchd", blocks, eye).reshape(n * k, n * k)


def kernel(x, mem, pre_norm_g, post_norm_g, w_in, b_in, conv_a, norm_a_g, pool_w, pool_scale, shift_mu_c,
           decay_w0, decay_w2, iclr_a0, iclr_a2, key_k, key_a, bonus_u, gn_c_g, gn_c_b, qk_norm_d, w_out,
           mem_norm_g, xattn_pre_g, xattn_post_g, xattn_wq, xattn_wkv, xattn_wo):
    bsz, seq, d = x.shape
    depth = w_in.shape[0]
    gw = GROUP_WIDTH

    w_in_r = _relayout_w(jnp.swapaxes(w_in, 1, 2))
    b_in_r = _relayout_in(b_in, 1).reshape(depth, 1, N_PROJ)
    gains = jnp.stack([pre_norm_g, post_norm_g, xattn_pre_g, xattn_post_g]
                      + [jnp.zeros_like(pre_norm_g)] * (_G_ROWS - 4), axis=1)
    mu = shift_mu_c
    zeros_row = jnp.zeros((depth, gw), F32)
    mu_wa = jnp.concatenate([mu[:, 3 * gw:], jnp.zeros((depth, gw - 2 * LORA), F32)], axis=-1)
    rows = [mu[:, 0:gw], mu[:, gw:2 * gw], mu[:, 2 * gw:3 * gw], decay_w0, iclr_a0, key_k, key_a, bonus_u,
            gn_c_g, gn_c_b, mu_wa] + [zeros_row] * (_P_ROWS - 11)
    rwkv_prm = jnp.stack(rows, axis=1)
    zl = jnp.zeros((depth, LORA, gw), F32)
    w2a2 = jnp.concatenate([jnp.concatenate([decay_w2, zl], axis=-1),
                            jnp.concatenate([zl, iclr_a2], axis=-1)], axis=1).astype(BF16)
    pool_bd = jax.vmap(_block_diag)(pool_w).astype(BF16)
    qkg = jnp.tile(qk_norm_d, (1, 1, N_HEADS))
    norm_a = norm_a_g.reshape(depth, 1, gw)
    pool_sc = pool_scale.reshape(depth, 1, gw)
    w_out_b = w_out.astype(BF16)
    wq_b = xattn_wq.astype(BF16)
    wo_b = xattn_wo.astype(BF16)

    kv_all = _mem_kv(mem, mem_norm_g.reshape(1, d), xattn_wkv.astype(BF16))

    x2 = x.reshape(bsz * seq, d)
    sh = lambda u: u.reshape(bsz, seq, u.shape[-1])
    for l in range(depth):
        layer = jnp.full((1,), l, jnp.int32)
        pa, pb, pc, pd, pg = _in_proj(x2, gains, w_in_r, b_in_r, layer)
        gates = sh(pg)
        y_a = _mlstm(sh(pa), gates, conv_a, norm_a, layer)
        y_b = _pool(sh(pb), pool_bd, pool_sc, layer)
        y_c = _rwkv(sh(pc), rwkv_prm, w2a2, layer)
        y_d = _fox(sh(pd), gates, qkg, layer)
        ys = [u.reshape(bsz * seq, gw) for u in (y_a, y_b, y_c, y_d)]
        x2 = _out_xattn(x2, ys, w_out_b, gains, wq_b, kv_all, wo_b, seq, layer)
    return x2.reshape(bsz, seq, d)
```

```python
import functools

import jax
import jax.numpy as jnp
from jax import lax
from jax.experimental import pallas as pl
from jax.experimental.pallas import tpu as pltpu

F32 = jnp.float32
BF16 = jnp.bfloat16

HEAD_DIM = 64
N_HEADS = 4
GROUP_WIDTH = HEAD_DIM * N_HEADS
LANES = 128
SUBLANES = 8
BF16_ROWS = 16
NORM_EPS = 1e-6
GN_EPS = 64e-5
LOG2_E = 1.4426950408889634
POOL_WINDOWS = (2, 4, 8, 16)
CONV_WIDTH_A = 4
LORA = 64
MLSTM_CHUNK = 128
MLSTM_GROUP = 4
RWKV_CHUNK = 64
RWKV_SUB = 16
RWKV_GROUP = 8
RWKV_PAIR = 2
DECAY_SCALE = 0.6065306597126334
FOX_QBLOCK = 256
ROW_TILE = 1024
XATTN_ROW_TILE = 1024
XATTN_PARTS = 4
RELAYOUT_COLS = 256
VMEM_LIMIT = 56 * 1024 * 1024

SEG_A = 4 * GROUP_WIDTH
SEG_B = 2 * GROUP_WIDTH
SEG_C = 4 * GROUP_WIDTH + 2 * LORA
SEG_D = 4 * GROUP_WIDTH
SEG_G = LANES
N_PROJ = SEG_A + SEG_B + SEG_C + SEG_D + SEG_G
GATE_I, GATE_F, GATE_D = 0, N_HEADS, 2 * N_HEADS

(_G_PRE, _G_POST, _G_XPRE, _G_XPOST) = range(4)
_G_ROWS = 8

(_P_MU_R, _P_MU_K, _P_MU_V, _P_W0, _P_A0, _P_KEY_K, _P_KEY_A, _P_BONUS, _P_GN_G, _P_GN_B, _P_MU_WA) = range(11)
_P_ROWS = 16


def _lane(shape):
    return lax.broadcasted_iota(jnp.int32, shape, len(shape) - 1)


def _row(shape):
    return lax.broadcasted_iota(jnp.int32, shape, len(shape) - 2)


def _nt(a, b):
    return lax.dot_general(a, b, (((1,), (1,)), ((), ())), preferred_element_type=F32)


def _tn(a, b):
    return lax.dot_general(a, b, (((0,), (0,)), ((), ())), preferred_element_type=F32)


def _nn(a, b):
    return jnp.dot(a, b, preferred_element_type=F32)


def _sigmoid(x):
    return 1.0 / (1.0 + jnp.exp(-x))


def _silu(x):
    return x * _sigmoid(x)


def _softplus(x):
    return jnp.maximum(x, 0.0) + jnp.log(1.0 + jnp.exp(-jnp.abs(x)))


def _log_sigmoid(x):
    return -_softplus(-x)


def _shift_rows(x, s):
    rolled = pltpu.roll(x, shift=s, axis=0)
    return jnp.where(_row(x.shape) >= s, rolled, 0.0)


def _cumsum_rows(x, seg):
    pos = _row(x.shape) & (seg - 1)
    s = 1
    while s < seg:
        x = x + jnp.where(pos >= s, pltpu.roll(x, shift=s, axis=0), 0.0)
        s *= 2
    return x


def _cummax_rows(x, seg):
    pos = _row(x.shape) & (seg - 1)
    s = 1
    while s < seg:
        x = jnp.maximum(x, jnp.where(pos >= s, pltpu.roll(x, shift=s, axis=0), -jnp.inf))
        s *= 2
    return x


def _stack_heads(x):
    hid = _lane(x.shape) >> 6
    zero = jnp.zeros_like(x)
    return jnp.concatenate([jnp.where(hid == h, x, zero) for h in range(N_HEADS)], axis=0)


def _per_head(cols, shape):
    hid = _lane(shape) >> 6
    out = jnp.broadcast_to(cols[N_HEADS - 1], shape)
    for h in range(N_HEADS - 2, -1, -1):
        out = jnp.where(hid == h, cols[h], out)
    return out


def _per_head_rows(vals, shape):
    hid = _row(shape) >> 6
    out = jnp.broadcast_to(vals[N_HEADS - 1], shape)
    for h in range(N_HEADS - 2, -1, -1):
        out = jnp.where(hid == h, vals[h], out)
    return out


def _head_cols(x, first):
    return [x[:, first + h:first + h + 1] for h in range(N_HEADS)]


def _block_ones(n):
    return jnp.where((_row((n, n)) >> 6) == (_lane((n, n)) >> 6), 1.0, 0.0).astype(BF16)


def _group_sum(x, ones_bd):
    hi = x.astype(BF16)
    lo = (x - hi.astype(F32)).astype(BF16)
    return _nn(hi, ones_bd) + _nn(lo, ones_bd)


def _rms_rows(x, g):
    return x * lax.rsqrt(jnp.mean(x * x, axis=-1, keepdims=True) + NORM_EPS) * g


def _params(**kw):
    return pltpu.CompilerParams(vmem_limit_bytes=VMEM_LIMIT, **kw)


def _layer_spec(shape):
    zeros = (0,) * len(shape)
    return pl.BlockSpec((None,) + tuple(shape), lambda i, layer: (layer[0],) + zeros)


def _row_spec(shape):
    zeros = (0,) * (len(shape) - 1)
    return pl.BlockSpec(tuple(shape), lambda i, layer: (i,) + zeros)


def _layered_call(body, grid, in_specs, out_specs, out_shape, scratch_shapes=(), name=None):
    def with_layer(layer_ref, *refs):
        del layer_ref
        body(*refs)

    return pl.pallas_call(
        with_layer,
        grid_spec=pltpu.PrefetchScalarGridSpec(num_scalar_prefetch=1, grid=grid, in_specs=in_specs,
                                               out_specs=out_specs, scratch_shapes=scratch_shapes),
        out_shape=out_shape, compiler_params=_params(), name=name)


def _in_proj_kernel(x_ref, g_ref, w_ref, b_ref, oa_ref, ob_ref, oc_ref, od_ref, og_ref):
    h = _rms_rows(x_ref[...], g_ref[_G_PRE:_G_PRE + 1, :]).astype(BF16)
    off = 0
    for o_ref, width in ((oa_ref, SEG_A), (ob_ref, SEG_B), (oc_ref, SEG_C), (od_ref, SEG_D), (og_ref, SEG_G)):
        acc = _nt(h, w_ref[off:off + width, :]) + b_ref[:, off:off + width]
        o_ref[...] = acc.astype(o_ref.dtype)
        off += width


def _in_proj(x2, gains, w, b, layer):
    m, d = x2.shape
    tm = min(ROW_TILE, m)
    widths = (SEG_A, SEG_B, SEG_C, SEG_D, SEG_G)
    dtypes = (BF16, BF16, BF16, BF16, F32)
    return _layered_call(
        _in_proj_kernel,
        grid=(m // tm,),
        in_specs=[_row_spec((tm, d)), _layer_spec((_G_ROWS, d)), _layer_spec((N_PROJ, d)), _layer_spec((1, N_PROJ))],
        out_specs=[_row_spec((tm, wd)) for wd in widths],
        out_shape=[jax.ShapeDtypeStruct((m, wd), dt) for wd, dt in zip(widths, dtypes)],
        name="in_proj",
    )(layer, x2, gains, w, b)


def _mlstm_kernel(a_ref, g_ref, conv_ref, ng_ref, o_ref,
                  q_s, k_s, bc_s, gt_s, bt_s, cl_s, nl_s, ml_s):
    s_len = a_ref.shape[0]
    L = MLSTM_CHUNK
    nc = s_len // L
    grp = MLSTM_GROUP
    gw = GROUP_WIDTH

    rows_of = lambda c: pl.ds(pl.multiple_of(c * L, L), L)

    def conv_and_gates(c, carry):
        r0 = pl.multiple_of(c * L, L)
        halo = a_ref[pl.ds(pl.multiple_of(jnp.maximum(r0 - BF16_ROWS, 0), BF16_ROWS), BF16_ROWS), :2 * gw]
        halo = halo.astype(F32) * jnp.where(r0 > 0, 1.0, 0.0)
        qk = jnp.concatenate([halo, a_ref[rows_of(c), :2 * gw].astype(F32)], axis=0)
        acc = qk * conv_ref[CONV_WIDTH_A - 1:CONV_WIDTH_A, :]
        for j in range(1, CONV_WIDTH_A):
            acc = acc + pltpu.roll(qk, shift=j, axis=0) * conv_ref[CONV_WIDTH_A - 1 - j:CONV_WIDTH_A - j, :]
        acc = _silu(acc[BF16_ROWS:, :])
        q_s[rows_of(c), :] = acc[:, :gw].astype(BF16)
        k_s[rows_of(c), :] = (acc[:, gw:] * (HEAD_DIM ** -0.5)).astype(BF16)
        gates = g_ref[rows_of(c), :]
        bcum = _cumsum_rows(_log_sigmoid(gates), L)
        bc_s[rows_of(c), :] = bcum
        gt_s[c] = jnp.transpose(gates)
        bt_s[c] = jnp.transpose(bcum)
        return carry

    ones_bd = _block_ones(gw)
    ones_rows = jnp.where((_row((N_HEADS * L, gw)) // L) == (_lane((N_HEADS * L, gw)) >> 6), 1.0, 0.0).astype(BF16)
    tril = _lane((L, L)) <= _row((L, L))
    bd_mask = (_row((gw, gw)) >> 6) == (_lane((gw, gw)) >> 6)
    rows8 = lambda u: jnp.broadcast_to(u, (SUBLANES, u.shape[-1]))

    def local_state(i, carry):
        cs = [i * grp + j for j in range(grp)]
        for c in cs:
            conv_and_gates(c, carry)
        ks = [k_s[rows_of(c), :] for c in cs]
        e_fulls, ves = [], []
        for c in cs:
            v = a_ref[rows_of(c), 2 * gw:3 * gw]
            ig = pltpu.roll(g_ref[rows_of(c), :], shift=GATE_F - GATE_I, axis=1)
            w_loc = bc_s[pl.ds(pl.multiple_of(c * L, L) + L - 1, 1), :] - bc_s[rows_of(c), :] + ig
            m_loc = jnp.max(w_loc, axis=0, keepdims=True)
            ml_s[c] = rows8(m_loc)
            e_full = _per_head(_head_cols(jnp.exp(w_loc - m_loc), GATE_F), (L, gw))
            e_fulls.append(e_full)
            ves.append((v.astype(F32) * e_full).astype(BF16))
        c_locs = [_tn(ve, k) for ve, k in zip(ves, ks)]
        for c, c_loc, e_full, k in zip(cs, c_locs, e_fulls, ks):
            cl_s[c] = jnp.where(bd_mask, c_loc, 0.0)
            nl_s[c] = rows8(jnp.sum(e_full * k.astype(F32), axis=0, keepdims=True))
        return carry

    lax.fori_loop(0, nc // grp, local_state, 0)

    def scan(c, carry):
        c_prev, n_prev, m_prev = carry
        g_tot = bc_s[pl.ds(pl.multiple_of(c * L, L) + L - 1, 1), :]
        m_loc = ml_s[c][0:1, :]
        m_new = jnp.maximum(g_tot + m_prev, m_loc)
        a_row = jnp.exp(g_tot + m_prev - m_new)
        b_row = jnp.exp(m_loc - m_new)
        a_h, b_h = _head_cols(a_row, GATE_F), _head_cols(b_row, GATE_F)
        c_new = _per_head_rows(a_h, (gw, gw)) * c_prev + _per_head_rows(b_h, (gw, gw)) * cl_s[c]
        n_new = _per_head(a_h, (1, gw)) * n_prev + _per_head(b_h, (1, gw)) * nl_s[c][0:1, :]
        return c_new, n_new, m_new

    def outputs(i, carry):
        cs = [i * grp + j for j in range(grp)]
        states = [carry]
        for c in cs:
            states.append(scan(c, states[-1]))
        hid = _lane((L, gw)) >> 6
        qs = [q_s[rows_of(c), :] for c in cs]
        ks = [k_s[rows_of(c), :] for c in cs]
        vs = [a_ref[rows_of(c), 2 * gw:3 * gw] for c in cs]
        logits = [[_nt(jnp.where(hid == h, q, jnp.zeros_like(q)), k) for h in range(N_HEADS)] for q, k in zip(qs, ks)]
        num_inter = [_nt(q, st[0].astype(BF16)) for q, st in zip(qs, states)]
        den_inter = [_group_sum(q.astype(F32) * st[1], ones_bd) for q, st in zip(qs, states)]
        shape = (L, gw)
        s_cat, iws, floors = [], [], []
        for j, c in enumerate(cs):
            b_col = bc_s[rows_of(c), :]
            g_row = gt_s[c]
            b_row = bt_s[c]
            ig = pltpu.roll(g_ref[rows_of(c), :], shift=GATE_F - GATE_I, axis=1)
            m_inter = b_col + states[j][2]
            m_t = jnp.maximum(m_inter, b_col + _cummax_rows(ig - b_col, L))
            iws.append(_per_head(_head_cols(jnp.exp(m_inter - m_t), GATE_F), shape))
            floors.append(_per_head(_head_cols(jnp.exp(-m_t), GATE_F), shape))
            col_arg = b_col - m_t
            s_list = []
            for h in range(N_HEADS):
                row_arg = g_row[GATE_I + h:GATE_I + h + 1, :] - b_row[GATE_F + h:GATE_F + h + 1, :]
                arg = jnp.where(tril, col_arg[:, GATE_F + h:GATE_F + h + 1] + row_arg, -jnp.inf)
                s_list.append((logits[j][h] * jnp.exp(arg)).astype(BF16))
            s_cat.append(jnp.concatenate(s_list, axis=1))
        num_intra = [_nn(s, _stack_heads(v)) for s, v in zip(s_cat, vs)]
        den_intra = [_nn(s, ones_rows) for s in s_cat]
        hhs = []
        for j in range(grp):
            num = num_intra[j] + iws[j] * num_inter[j]
            den = den_intra[j] + iws[j] * den_inter[j]
            hhs.append(num / jnp.maximum(jnp.abs(den), floors[j]))
        mss = [_group_sum(hh * hh, ones_bd) for hh in hhs]
        for c, hh, ms in zip(cs, hhs, mss):
            z = a_ref[rows_of(c), 3 * gw:4 * gw].astype(F32)
            y = hh * lax.rsqrt(ms * (1.0 / HEAD_DIM) + NORM_EPS) * ng_ref[...] * _silu(z)
            o_ref[rows_of(c), :] = y.astype(o_ref.dtype)
        return states[-1]

    lax.fori_loop(0, nc // grp, outputs,
                  (jnp.zeros((gw, gw), F32), jnp.zeros((1, gw), F32), jnp.zeros((1, LANES), F32)))


def _mlstm(a, gates, conv, ng, layer):
    b, s, _ = a.shape
    gw = GROUP_WIDTH
    L = MLSTM_CHUNK
    nc = s // L
    assert nc % MLSTM_GROUP == 0
    return _layered_call(
        _mlstm_kernel,
        grid=(b,),
        in_specs=[_row_spec((None, s, SEG_A)), _row_spec((None, s, SEG_G)),
                  _layer_spec((CONV_WIDTH_A, 2 * gw)), _layer_spec((1, gw))],
        out_specs=_row_spec((None, s, gw)),
        out_shape=jax.ShapeDtypeStruct((b, s, gw), BF16),
        scratch_shapes=[
            pltpu.VMEM((s, gw), BF16), pltpu.VMEM((s, gw), BF16),
            pltpu.VMEM((s, SEG_G), F32),
            pltpu.VMEM((nc, SEG_G, L), F32), pltpu.VMEM((nc, SEG_G, L), F32),
            pltpu.VMEM((nc, gw, gw), F32), pltpu.VMEM((nc, SUBLANES, gw), F32), pltpu.VMEM((nc, SUBLANES, LANES), F32),
        ],
        name="mlstm",
    )(layer, a, gates, conv, ng)


def _pool_kernel(b_ref, w_ref, sc_ref, o_ref):
    gw = GROUP_WIDTH
    u = b_ref[:, :gw].astype(F32)
    z = b_ref[:, gw:].astype(F32)
    head = POOL_WINDOWS[-1]
    grp = _lane((u.shape[0] + head, gw)) >> 6
    lane_grp = _lane((1, gw)) >> 6
    run, width = jnp.concatenate([jnp.zeros((head, gw), F32), u], axis=0), 1
    window = jnp.full((1, gw), float(POOL_WINDOWS[-1]), F32)
    sums = []
    for win in POOL_WINDOWS:
        while width < win:
            run = run + pltpu.roll(run, shift=width, axis=0)
            width *= 2
        sums.append(run)
    window_sum = sums[-1]
    for gi in range(len(POOL_WINDOWS) - 2, -1, -1):
        window_sum = jnp.where(grp == gi, sums[gi], window_sum)
        window = jnp.where(lane_grp == gi, float(POOL_WINDOWS[gi]), window)
    window_sum = window_sum[head:, :]
    pos = _row((head, gw)).astype(F32)
    first = window_sum[:head, :] / jnp.minimum(pos + 1.0, window) - u[:head, :]
    rest = window_sum[head:, :] * (1.0 / window) - u[head:, :]
    pooled = jnp.concatenate([first, rest], axis=0)
    mixed = _nn(pooled.astype(BF16), w_ref[...])
    o_ref[...] = (mixed * sc_ref[...] * _silu(z)).astype(o_ref.dtype)


def _pool(bx, w_bd, scale, layer):
    b, s, _ = bx.shape
    gw = GROUP_WIDTH
    return _layered_call(
        _pool_kernel,
        grid=(b,),
        in_specs=[_row_spec((None, s, SEG_B)), _layer_spec((gw, gw)), _layer_spec((1, gw))],
        out_specs=_row_spec((None, s, gw)),
        out_shape=jax.ShapeDtypeStruct((b, s, gw), BF16),
        name="pool",
    )(layer, bx, w_bd, scale)


def _cat_mm(xs, ys):
    return [_nn(x.astype(BF16), _stack_heads(y.astype(BF16))) for x, y in zip(xs, ys)]


def _unit_lower_inverse(mats):
    t = RWKV_CHUNK
    shift = RWKV_SUB.bit_length() - 1
    shape = mats[0].shape
    rb = _row(shape) >> shift
    col = _lane(shape) & (t - 1)
    cb = col >> shift
    eye = jnp.where(col == _row(shape), 1.0, 0.0)
    add = lambda xs, ys: [x + y for x, y in zip(xs, ys)]
    d = [jnp.where(rb == cb, a, 0.0) for a in mats]
    e = [jnp.where(rb == cb, 0.0, a) for a in mats]
    d2 = _cat_mm(d, d)
    d4 = _cat_mm(d2, d2)
    x = [eye + di for di in d]
    x = add(x, _cat_mm(x, d2))
    d8 = _cat_mm(d4, d4)
    x = add(x, _cat_mm(x, d4))
    dinv = add(x, _cat_mm(x, d8))
    f = _cat_mm(dinv, e)
    f2 = _cat_mm(f, f)
    w = [eye + fi for fi in f]
    w = add(w, _cat_mm(w, f2))
    return _cat_mm(w, dinv)


def _rwkv_kernel(c_ref, p_ref, w2a2_ref, o_ref,
                 v_s, bonus_s, pr_s, qk_s, ul_s, ar_s, ge_s, ub_s, rs_s):
    n_seq, s_len = c_ref.shape[0], c_ref.shape[1]
    gw = GROUP_WIDTH
    T = RWKV_CHUNK
    grp = RWKV_GROUP
    rows = T * grp
    nc = s_len // T
    prm = lambda i: p_ref[i:i + 1, :]

    ones_bd = _block_ones(gw)
    col = _lane((T, N_HEADS * T)) & (T - 1)
    row = _row((T, N_HEADS * T))
    strict = col < row
    incl = col <= row
    bd_mask = (_row((gw, gw)) >> 6) == (_lane((gw, gw)) >> 6)
    first_row = _row((rows, gw)) == 0

    def prepare(e, i, carry):
        r0 = pl.multiple_of(i * rows, rows)
        tile = pl.ds(r0, rows)
        halo = c_ref[e, pl.ds(pl.multiple_of(jnp.maximum(r0 - BF16_ROWS, 0), BF16_ROWS), BF16_ROWS), :]
        prev = halo[BF16_ROWS - 1:BF16_ROWS, :].astype(F32) * jnp.where(r0 > 0, 1.0, 0.0)

        def shifted(lo, hi, mu):
            u = c_ref[e, tile, lo:hi].astype(F32)
            before = jnp.where(first_row[:, :hi - lo], prev[:, lo:hi], pltpu.roll(u, shift=1, axis=0))
            return u + mu * (before - u)

        wa = shifted(4 * gw, 4 * gw + 2 * LORA, p_ref[_P_MU_WA:_P_MU_WA + 1, :2 * LORA])
        wa = jnp.where(_lane(wa.shape) < LORA, jnp.tanh(wa), wa)
        lora = _nn(wa.astype(BF16), w2a2_ref[...])
        ld = -DECAY_SCALE * _sigmoid(prm(_P_W0) + lora[:, :gw])
        a = _sigmoid(prm(_P_A0) + lora[:, gw:])
        r = shifted(0, gw, prm(_P_MU_R))
        k = shifted(gw, 2 * gw, prm(_P_MU_K))
        v = shifted(2 * gw, 3 * gw, prm(_P_MU_V))
        kk = k * prm(_P_KEY_K)
        kk = kk / jnp.maximum(jnp.sqrt(_group_sum(kk * kk, ones_bd)), 1e-12)
        k = k * (1.0 + (a - 1.0) * prm(_P_KEY_A))
        bonus_s[e, tile, :] = (_group_sum(r * k * prm(_P_BONUS), ones_bd) * v).astype(BF16)
        vb = v.astype(BF16)
        v_s[e, tile, :] = vb

        lc = _cumsum_rows(ld, T)
        g_inc = jnp.exp(lc)
        g_inv = jnp.exp(-lc)
        pt = (-kk * jnp.exp(lc - ld)).astype(BF16)
        rt = (r * g_inc).astype(BF16)
        qt = (kk * a * g_inv).astype(BF16)
        kt = (k * g_inv).astype(BF16)

        part = lambda u, j: u[j * T:(j + 1) * T, :]
        chunks = range(grp)
        prs = [jnp.concatenate([part(pt, j), part(rt, j)], axis=0) for j in chunks]
        gs = [_nt(prs[j], jnp.concatenate([_stack_heads(part(qt, j)), _stack_heads(part(kt, j))], axis=0))
              for j in chunks]
        a_pq = [jnp.where(strict, g[:T, :N_HEADS * T], 0.0) for g in gs]
        a_pk = [jnp.where(strict, g[:T, N_HEADS * T:], 0.0).astype(BF16) for g in gs]
        apv = [_nn(a_pk[j], _stack_heads(part(vb, j))) for j in chunks]
        tinv = [t.astype(BF16) for t in _unit_lower_inverse(a_pq)]
        p_fold = [_nn(tinv[j], _stack_heads(part(pt, j))) for j in chunks]
        u_loc = [_nn(tinv[j], _stack_heads(apv[j].astype(BF16))) for j in chunks]
        for j in chunks:
            c = i * grp + j
            pr_s[e, c] = jnp.concatenate([p_fold[j].astype(BF16), part(rt, j)], axis=0)
            qk_s[e, c] = jnp.concatenate([part(qt, j), part(kt, j)], axis=0)
            ul_s[e, c] = u_loc[j].astype(BF16)
            ar_s[e, c] = jnp.concatenate([jnp.where(incl, gs[j][T:, :N_HEADS * T], 0.0),
                                          jnp.where(incl, gs[j][T:, N_HEADS * T:], 0.0)], axis=1).astype(BF16)
            ge_s[e, c] = jnp.broadcast_to(part(g_inc, j)[T - 1:T, :], (SUBLANES, gw))
        return carry

    for e in range(n_seq):
        lax.fori_loop(0, nc // grp, functools.partial(prepare, e), 0)

    seqs = range(n_seq)

    rows_of = lambda c: pl.ds(pl.multiple_of(c * T, T), T)

    def project(c, states):
        return [_nt(pr_s[e, c], states[e].astype(BF16)) for e in seqs]

    def correct(c, prs):
        ubs = [(prs[e][:T] + ul_s[e, c].astype(F32)).astype(BF16) for e in seqs]
        dss = [_tn(jnp.concatenate([ubs[e], v_s[e, rows_of(c), :]], axis=0), qk_s[e, c]) for e in seqs]
        for e in seqs:
            ub_s[e, c] = ubs[e]
            rs_s[e, c] = prs[e][T:].astype(BF16)
        return dss

    def advance(c, states, dss):
        return tuple((states[e] + jnp.where(bd_mask, dss[e], 0.0)) * ge_s[e, c][0:1, :] for e in seqs)

    def readout(c):
        return [rs_s[e, c].astype(F32)
                + _nn(ar_s[e, c], jnp.concatenate([_stack_heads(ub_s[e, c]), _stack_heads(v_s[e, rows_of(c), :])], axis=0))
                for e in seqs]

    def centre(ys):
        means = [_group_sum(y, ones_bd) * (1.0 / HEAD_DIM) for y in ys]
        return [y - m for y, m in zip(ys, means)]

    def finish(c, ycs):
        vrs = [_group_sum(yc * yc, ones_bd) * (1.0 / HEAD_DIM) for yc in ycs]
        for e in seqs:
            y = ycs[e] * lax.rsqrt(vrs[e] + GN_EPS) * prm(_P_GN_G) + prm(_P_GN_B) + bonus_s[e, rows_of(c), :].astype(F32)
            z = c_ref[e, rows_of(c), 3 * gw:4 * gw].astype(F32)
            o_ref[e, rows_of(c), :] = (y * _silu(z)).astype(o_ref.dtype)

    def step(c, states):
        prs = project(c, states)
        ys = readout(c - 1)
        dss = correct(c, prs)
        ycs = centre(ys)
        new_states = advance(c, states, dss)
        finish(c - 1, ycs)
        return new_states

    zero_states = tuple(jnp.zeros((gw, gw), F32) for _ in seqs)
    states = advance(0, zero_states, correct(0, project(0, zero_states)))
    lax.fori_loop(1, nc, step, states, unroll=2)
    finish(nc - 1, centre(readout(nc - 1)))


def _rwkv(cx, prm, w2a2, layer):
    b, s, _ = cx.shape
    gw = GROUP_WIDTH
    T = RWKV_CHUNK
    nc = s // T
    n_seq = RWKV_PAIR if b % RWKV_PAIR == 0 else 1
    assert nc % RWKV_GROUP == 0
    per_chunk = lambda r, w, dt: pltpu.VMEM((n_seq, nc, r, w), dt)
    return _layered_call(
        _rwkv_kernel,
        grid=(b // n_seq,),
        in_specs=[_row_spec((n_seq, s, SEG_C)), _layer_spec((_P_ROWS, gw)), _layer_spec((2 * LORA, 2 * gw))],
        out_specs=_row_spec((n_seq, s, gw)),
        out_shape=jax.ShapeDtypeStruct((b, s, gw), BF16),
        scratch_shapes=[
            pltpu.VMEM((n_seq, s, gw), BF16), pltpu.VMEM((n_seq, s, gw), BF16),
            per_chunk(2 * T, gw, BF16), per_chunk(2 * T, gw, BF16),
            per_chunk(T, gw, BF16),
            per_chunk(T, 2 * N_HEADS * T, BF16), per_chunk(SUBLANES, gw, F32),
            per_chunk(T, gw, BF16), per_chunk(T, gw, BF16),
        ],
        name="rwkv",
    )(layer, cx, prm, w2a2)


def _fox_kernel(d_ref, g_ref, qkg_ref, o_ref, q_s, k_s, vh_s, ft_s):
    s_len = d_ref.shape[0]
    gw = GROUP_WIDTH
    tq = min(FOX_QBLOCK, s_len)
    ones_bd = _block_ones(gw)

    def head_norm(u, g):
        ms = _group_sum(u * u, ones_bd) * (1.0 / HEAD_DIM)
        return u * lax.rsqrt(ms + NORM_EPS) * g

    q_s[...] = (head_norm(d_ref[:, 0:gw].astype(F32), qkg_ref[0:1, :]) * (LOG2_E * HEAD_DIM ** -0.5)).astype(BF16)
    k_s[...] = head_norm(d_ref[:, gw:2 * gw].astype(F32), qkg_ref[1:2, :]).astype(BF16)
    v = d_ref[:, 2 * gw:3 * gw]
    den_lane = [((h + 1) % N_HEADS) * HEAD_DIM for h in range(N_HEADS)]
    for h in range(N_HEADS):
        spare = jnp.where(_lane(v.shape) == den_lane[h], 1.0, 0.0).astype(BF16)
        vh_s[h] = jnp.where((_lane(v.shape) >> 6) == h, v, spare)
    ft_s[...] = jnp.transpose(_cumsum_rows(_log_sigmoid(g_ref[...]), s_len) * LOG2_E)

    tril = _lane((tq, tq)) <= _row((tq, tq))
    hid = _lane((tq, gw)) >> 6
    for blk in range(s_len // tq):
        q0, q1 = blk * tq, (blk + 1) * tq
        q = q_s[q0:q1, :]
        out = jnp.zeros((tq, gw), F32)
        for h in range(N_HEADS):
            qh = jnp.where(hid == h, q, jnp.zeros_like(q))
            f_row = ft_s[GATE_D + h:GATE_D + h + 1, :]
            l_diag = jnp.where(tril, _nt(qh, k_s[q0:q1, :]) - f_row[:, q0:q1], -jnp.inf)
            mx = jnp.max(l_diag, axis=-1, keepdims=True)
            if blk:
                l_off = _nt(qh, k_s[0:q0, :]) - f_row[:, 0:q0]
                mx = jnp.maximum(mx, jnp.max(l_off, axis=-1, keepdims=True))
            acc = _nn(jnp.exp2(l_diag - mx).astype(BF16), vh_s[h, q0:q1, :])
            if blk:
                acc = acc + _nn(jnp.exp2(l_off - mx).astype(BF16), vh_s[h, 0:q0, :])
            out = jnp.where(hid == h, acc * (1.0 / acc[:, den_lane[h]:den_lane[h] + 1]), out)
        z = d_ref[q0:q1, 3 * gw:4 * gw].astype(F32)
        o_ref[q0:q1, :] = (out * _silu(z)).astype(o_ref.dtype)


def _fox(dx, gates, qkg, layer):
    b, s, _ = dx.shape
    gw = GROUP_WIDTH
    return _layered_call(
        _fox_kernel,
        grid=(b,),
        in_specs=[_row_spec((None, s, SEG_D)), _row_spec((None, s, SEG_G)), _layer_spec((2, gw))],
        out_specs=_row_spec((None, s, gw)),
        out_shape=jax.ShapeDtypeStruct((b, s, gw), BF16),
        scratch_shapes=[pltpu.VMEM((s, gw), BF16), pltpu.VMEM((s, gw), BF16),
                        pltpu.VMEM((N_HEADS, s, gw), BF16), pltpu.VMEM((SEG_G, s), F32)],
        name="fox",
    )(layer, dx, gates, qkg)


def _mem_kv_kernel(mem_ref, g_ref, w_ref, o_ref):
    b, n_mem, d = mem_ref.shape
    mn = _rms_rows(mem_ref[...].reshape(b * n_mem, d), g_ref[...]).astype(BF16)
    o_ref[...] = _nn(mn, w_ref[...]).astype(o_ref.dtype).reshape(o_ref.shape)


def _mem_kv(mem, g, wkv):
    b, n_mem, d = mem.shape
    depth = wkv.shape[0]
    return pl.pallas_call(
        _mem_kv_kernel,
        grid=(depth,),
        in_specs=[
            pl.BlockSpec((b, n_mem, d), lambda l: (0, 0, 0)),
            pl.BlockSpec((1, d), lambda l: (0, 0)),
            pl.BlockSpec((None, d, 2 * d), lambda l: (l, 0, 0)),
        ],
        out_specs=pl.BlockSpec((None, b, n_mem, 2 * d), lambda l: (l, 0, 0, 0)),
        out_shape=jax.ShapeDtypeStruct((depth, b, n_mem, 2 * d), BF16),
        compiler_params=_params(),
        name="mem_kv",
    )(mem, g, wkv)


def _out_xattn_kernel(x_ref, ya_ref, yb_ref, yc_ref, yd_ref, wout_ref, g_ref, wq_ref, kv_ref, wo_ref,
                      o_ref, *, n_mem_heads):
    d = x_ref.shape[-1]
    dh = d // n_mem_heads
    tm = x_ref.shape[0]
    gain = lambda i: g_ref[i:i + 1, :]
    parts = [pl.ds(j * (tm // XATTN_PARTS), tm // XATTN_PARTS) for j in range(XATTN_PARTS)]
    ys = [jnp.concatenate([ya_ref[r, :], yb_ref[r, :], yc_ref[r, :], yd_ref[r, :]], axis=-1) for r in parts]
    mixed = [_nn(y, wout_ref[...]) for y in ys]
    xs = [x_ref[r, :] + _rms_rows(m, gain(_G_POST)) for r, m in zip(parts, mixed)]
    hms = [_rms_rows(x, gain(_G_XPRE)).astype(BF16) for x in xs]
    qs = [_nn(hm, wq_ref[...]).astype(BF16) for hm in hms]
    outs = [[] for _ in parts]
    for h in range(n_mem_heads):
        kh = kv_ref[:, h * dh:(h + 1) * dh]
        vh = kv_ref[:, d + h * dh:d + (h + 1) * dh]
        logits = [_nt(q[:, h * dh:(h + 1) * dh], kh) * (dh ** -0.5) for q in qs]
        ps = [jnp.exp(l - jnp.max(l, axis=-1, keepdims=True)) for l in logits]
        ps = [(p / jnp.sum(p, axis=-1, keepdims=True)).astype(BF16) for p in ps]
        for j, p in enumerate(ps):
            outs[j].append(_nn(p, vh).astype(BF16))
    attn = [_nn(jnp.concatenate(o, axis=-1), wo_ref[...]) for o in outs]
    for r, x, o in zip(parts, xs, attn):
        o_ref[r, :] = x + _rms_rows(o, gain(_G_XPOST))


def _out_xattn(x2, ys, wout, gains, wq, kv, wo, seq_len, layer, n_mem_heads=4):
    m, d = x2.shape
    gw = GROUP_WIDTH
    tm = min(XATTN_ROW_TILE, seq_len)
    per_seq = seq_len // tm
    n_mem = kv.shape[2]
    return _layered_call(
        functools.partial(_out_xattn_kernel, n_mem_heads=n_mem_heads),
        grid=(m // tm,),
        in_specs=[_row_spec((tm, d))] + [_row_spec((tm, gw))] * 4 + [
            _layer_spec((4 * gw, d)),
            _layer_spec((_G_ROWS, d)),
            _layer_spec((d, d)),
            pl.BlockSpec((None, None, n_mem, 2 * d), lambda i, layer: (layer[0], i // per_seq, 0, 0)),
            _layer_spec((d, d)),
        ],
        out_specs=_row_spec((tm, d)),
        out_shape=jax.ShapeDtypeStruct((m, d), F32),
        name="out_xattn",
    )(layer, x2, *ys, wout, gains, wq, kv, wo)


_IN_SIZES = (("a_q", GROUP_WIDTH), ("a_k", GROUP_WIDTH), ("a_v", GROUP_WIDTH), ("a_i", N_HEADS), ("a_f", N_HEADS),
             ("a_z", GROUP_WIDTH), ("b_x", GROUP_WIDTH), ("b_z", GROUP_WIDTH),
             ("c_r", GROUP_WIDTH), ("c_k", GROUP_WIDTH), ("c_v", GROUP_WIDTH), ("c_w", LORA), ("c_a", LORA),
             ("c_z", GROUP_WIDTH), ("d_q", GROUP_WIDTH), ("d_k", GROUP_WIDTH), ("d_v", GROUP_WIDTH),
             ("d_f", N_HEADS), ("d_z", GROUP_WIDTH))
_IN_ORDER = ("a_q", "a_k", "a_v", "a_z", "b_x", "b_z", "c_r", "c_k", "c_v", "c_z", "c_w", "c_a",
             "d_q", "d_k", "d_v", "d_z", "a_i", "a_f", "d_f")
N_IN = sum(size for _, size in _IN_SIZES)


def _relayout_in(w, axis):
    pieces, off = {}, 0
    for name, size in _IN_SIZES:
        pieces[name] = lax.slice_in_dim(w, off, off + size, axis=axis)
        off += size
    pad_shape = list(w.shape)
    pad_shape[axis] = SEG_G - 3 * N_HEADS
    return jnp.concatenate([pieces[n] for n in _IN_ORDER] + [jnp.zeros(pad_shape, w.dtype)], axis=axis)


def _relayout_w_kernel(w_ref, o_ref):
    src = {}
    off = 0
    for name, size in _IN_SIZES:
        src[name] = (off, size)
        off += size
    dst = 0
    gate_rows = []
    for name in _IN_ORDER:
        start, size = src[name]
        if size < BF16_ROWS:
            gate_rows.append(w_ref[start:start + size, :])
        else:
            o_ref[dst:dst + size, :] = w_ref[start:start + size, :].astype(o_ref.dtype)
            dst += size
    used = sum(g.shape[0] for g in gate_rows)
    gate_rows.append(jnp.zeros((o_ref.shape[0] - dst - used, o_ref.shape[1]), F32))
    o_ref[dst:, :] = jnp.concatenate(gate_rows, axis=0).astype(o_ref.dtype)


def _relayout_w(w_t):
    depth, n_in, d = w_t.shape
    assert n_in == N_IN
    tc = RELAYOUT_COLS
    return pl.pallas_call(
        _relayout_w_kernel,
        grid=(depth, d // tc),
        in_specs=[pl.BlockSpec((None, n_in, tc), lambda l, i: (l, 0, i))],
        out_specs=pl.BlockSpec((None, N_PROJ, tc), lambda l, i: (l, 0, i)),
        out_shape=jax.ShapeDtypeStruct((depth, N_PROJ, d), BF16),
        compiler_params=_params(),
        name="relayout_w",
    )(w_t)


def _block_diag(blocks):
    n, k, _ = blocks.shape
    eye = jnp.eye(n, dtype=blocks.dtype)
    return jnp.einsum("gcd,gh->gchd", blocks, eye).reshape(n * k, n * k)


def kernel(x, mem, pre_norm_g, post_norm_g, w_in, b_in, conv_a, norm_a_g, pool_w, pool_scale, shift_mu_c,
           decay_w0, decay_w2, iclr_a0, iclr_a2, key_k, key_a, bonus_u, gn_c_g, gn_c_b, qk_norm_d, w_out,
           mem_norm_g, xattn_pre_g, xattn_post_g, xattn_wq, xattn_wkv, xattn_wo):
    bsz, seq, d = x.shape
    depth = w_in.shape[0]
    gw = GROUP_WIDTH

    w_in_r = _relayout_w(jnp.swapaxes(w_in, 1, 2))
    b_in_r = _relayout_in(b_in, 1).reshape(depth, 1, N_PROJ)
    gains = jnp.stack([pre_norm_g, post_norm_g, xattn_pre_g, xattn_post_g]
                      + [jnp.zeros_like(pre_norm_g)] * (_G_ROWS - 4), axis=1)
    mu = shift_mu_c
    zeros_row = jnp.zeros((depth, gw), F32)
    mu_wa = jnp.concatenate([mu[:, 3 * gw:], jnp.zeros((depth, gw - 2 * LORA), F32)], axis=-1)
    rows = [mu[:, 0:gw], mu[:, gw:2 * gw], mu[:, 2 * gw:3 * gw], decay_w0, iclr_a0, key_k, key_a, bonus_u,
            gn_c_g, gn_c_b, mu_wa] + [zeros_row] * (_P_ROWS - 11)
    rwkv_prm = jnp.stack(rows, axis=1)
    zl = jnp.zeros((depth, LORA, gw), F32)
    w2a2 = jnp.concatenate([jnp.concatenate([decay_w2, zl], axis=-1),
                            jnp.concatenate([zl, iclr_a2], axis=-1)], axis=1).astype(BF16)
    pool_bd = jax.vmap(_block_diag)(pool_w).astype(BF16)
    qkg = jnp.tile(qk_norm_d, (1, 1, N_HEADS))
    norm_a = norm_a_g.reshape(depth, 1, gw)
    pool_sc = pool_scale.reshape(depth, 1, gw)
    w_out_b = w_out.astype(BF16)
    wq_b = xattn_wq.astype(BF16)
    wo_b = xattn_wo.astype(BF16)

    kv_all = _mem_kv(mem, mem_norm_g.reshape(1, d), xattn_wkv.astype(BF16))

    x2 = x.reshape(bsz * seq, d)
    sh = lambda u: u.reshape(bsz, seq, u.shape[-1])
    for l in range(depth):
        layer = jnp.full((1,), l, jnp.int32)
        pa, pb, pc, pd, pg = _in_proj(x2, gains, w_in_r, b_in_r, layer)
        gates = sh(pg)
        y_a = _mlstm(sh(pa), gates, conv_a, norm_a, layer)
        y_b = _pool(sh(pb), pool_bd, pool_sc, layer)
        y_c = _rwkv(sh(pc), rwkv_prm, w2a2, layer)
        y_d = _fox(sh(pd), gates, qkg, layer)
        ys = [u.reshape(bsz * seq, gw) for u in (y_a, y_b, y_c, y_d)]
        x2 = _out_xattn(x2, ys, w_out_b, gains, wq_b, kv_all, wo_b, seq, layer)
    return x2.reshape(bsz, seq, d)
```

```python
import functools

import jax
import jax.numpy as jnp
from jax import lax
from jax.experimental import pallas as pl
from jax.experimental.pallas import tpu as pltpu

F32 = jnp.float32
BF16 = jnp.bfloat16

HEAD_DIM = 64
N_HEADS = 4
GROUP_WIDTH = HEAD_DIM * N_HEADS
LANES = 128
SUBLANES = 8
BF16_ROWS = 16
NORM_EPS = 1e-6
GN_EPS = 64e-5
LOG2_E = 1.4426950408889634
POOL_WINDOWS = (2, 4, 8, 16)
CONV_WIDTH_A = 4
LORA = 64
MLSTM_CHUNK = 128
MLSTM_GROUP = 4
RWKV_CHUNK = 64
RWKV_SUB = 16
RWKV_GROUP = 8
RWKV_STEP_UNROLL = 5
RWKV_PAIR = 2
DECAY_SCALE = 0.6065306597126334
FOX_QBLOCK = 256
ROW_TILE = 1024
XATTN_ROW_TILE = 1024
XATTN_PARTS = 4
RELAYOUT_COLS = 256
VMEM_LIMIT = 56 * 1024 * 1024

SEG_A = 4 * GROUP_WIDTH
SEG_B = 2 * GROUP_WIDTH
SEG_C = 4 * GROUP_WIDTH + 2 * LORA
SEG_D = 4 * GROUP_WIDTH
SEG_G = LANES
N_PROJ = SEG_A + SEG_B + SEG_C + SEG_D + SEG_G
GATE_I, GATE_F, GATE_D = 0, N_HEADS, 2 * N_HEADS

(_G_PRE, _G_POST, _G_XPRE, _G_XPOST) = range(4)
_G_ROWS = 8

(_P_MU_R, _P_MU_K, _P_MU_V, _P_W0, _P_A0, _P_KEY_K, _P_KEY_A, _P_BONUS, _P_GN_G, _P_GN_B, _P_MU_WA) = range(11)
_P_ROWS = 16


def _lane(shape):
    return lax.broadcasted_iota(jnp.int32, shape, len(shape) - 1)


def _row(shape):
    return lax.broadcasted_iota(jnp.int32, shape, len(shape) - 2)


def _nt(a, b):
    return lax.dot_general(a, b, (((1,), (1,)), ((), ())), preferred_element_type=F32)


def _tn(a, b):
    return lax.dot_general(a, b, (((0,), (0,)), ((), ())), preferred_element_type=F32)


def _nn(a, b):
    return jnp.dot(a, b, preferred_element_type=F32)


def _sigmoid(x):
    return 1.0 / (1.0 + jnp.exp(-x))


def _silu(x):
    return x * _sigmoid(x)


def _softplus(x):
    return jnp.maximum(x, 0.0) + jnp.log(1.0 + jnp.exp(-jnp.abs(x)))


def _log_sigmoid(x):
    return -_softplus(-x)


def _shift_rows(x, s):
    rolled = pltpu.roll(x, shift=s, axis=0)
    return jnp.where(_row(x.shape) >= s, rolled, 0.0)


def _cumsum_rows(x, seg):
    pos = _row(x.shape) & (seg - 1)
    s = 1
    while s < seg:
        x = x + jnp.where(pos >= s, pltpu.roll(x, shift=s, axis=0), 0.0)
        s *= 2
    return x


def _cummax_rows(x, seg):
    pos = _row(x.shape) & (seg - 1)
    s = 1
    while s < seg:
        x = jnp.maximum(x, jnp.where(pos >= s, pltpu.roll(x, shift=s, axis=0), -jnp.inf))
        s *= 2
    return x


def _stack_heads(x):
    hid = _lane(x.shape) >> 6
    zero = jnp.zeros_like(x)
    return jnp.concatenate([jnp.where(hid == h, x, zero) for h in range(N_HEADS)], axis=0)


def _per_head(cols, shape):
    hid = _lane(shape) >> 6
    out = jnp.broadcast_to(cols[N_HEADS - 1], shape)
    for h in range(N_HEADS - 2, -1, -1):
        out = jnp.where(hid == h, cols[h], out)
    return out


def _head_cols(x, first):
    return [x[:, first + h:first + h + 1] for h in range(N_HEADS)]


def _block_ones(n):
    return jnp.where((_row((n, n)) >> 6) == (_lane((n, n)) >> 6), 1.0, 0.0).astype(BF16)


def _group_sum(x, ones_bd):
    hi = x.astype(BF16)
    lo = (x - hi.astype(F32)).astype(BF16)
    return _nn(hi, ones_bd) + _nn(lo, ones_bd)


def _rms_rows(x, g):
    return x * lax.rsqrt(jnp.mean(x * x, axis=-1, keepdims=True) + NORM_EPS) * g


def _params(**kw):
    return pltpu.CompilerParams(vmem_limit_bytes=VMEM_LIMIT, **kw)


def _layer_spec(shape):
    zeros = (0,) * len(shape)
    return pl.BlockSpec((None,) + tuple(shape), lambda i, layer: (layer[0],) + zeros)


def _row_spec(shape):
    zeros = (0,) * (len(shape) - 1)
    return pl.BlockSpec(tuple(shape), lambda i, layer: (i,) + zeros)


def _layered_call(body, grid, in_specs, out_specs, out_shape, scratch_shapes=(), name=None):
    def with_layer(layer_ref, *refs):
        del layer_ref
        body(*refs)

    return pl.pallas_call(
        with_layer,
        grid_spec=pltpu.PrefetchScalarGridSpec(num_scalar_prefetch=1, grid=grid, in_specs=in_specs,
                                               out_specs=out_specs, scratch_shapes=scratch_shapes),
        out_shape=out_shape, compiler_params=_params(), name=name)


def _in_proj_kernel(x_ref, g_ref, w_ref, b_ref, oa_ref, ob_ref, oc_ref, od_ref, og_ref):
    h = _rms_rows(x_ref[...], g_ref[_G_PRE:_G_PRE + 1, :]).astype(BF16)
    off = 0
    for o_ref, width in ((oa_ref, SEG_A), (ob_ref, SEG_B), (oc_ref, SEG_C), (od_ref, SEG_D), (og_ref, SEG_G)):
        acc = _nt(h, w_ref[off:off + width, :]) + b_ref[:, off:off + width]
        o_ref[...] = acc.astype(o_ref.dtype)
        off += width


def _in_proj(x2, gains, w, b, layer):
    m, d = x2.shape
    tm = min(ROW_TILE, m)
    widths = (SEG_A, SEG_B, SEG_C, SEG_D, SEG_G)
    dtypes = (BF16, BF16, BF16, BF16, F32)
    return _layered_call(
        _in_proj_kernel,
        grid=(m // tm,),
        in_specs=[_row_spec((tm, d)), _layer_spec((_G_ROWS, d)), _layer_spec((N_PROJ, d)), _layer_spec((1, N_PROJ))],
        out_specs=[_row_spec((tm, wd)) for wd in widths],
        out_shape=[jax.ShapeDtypeStruct((m, wd), dt) for wd, dt in zip(widths, dtypes)],
        name="in_proj",
    )(layer, x2, gains, w, b)


def _mlstm_kernel(a_ref, g_ref, conv_ref, ng_ref, o_ref,
                  q_s, k_s, bc_s, gt_s, bt_s, cl_s, nl_s, ml_s):
    s_len = a_ref.shape[0]
    L = MLSTM_CHUNK
    nc = s_len // L
    grp = MLSTM_GROUP
    gw = GROUP_WIDTH

    rows_of = lambda c: pl.ds(pl.multiple_of(c * L, L), L)
    taps = CONV_WIDTH_A - 1
    shift_row = _row((taps * L, 2 * L))
    shift_mats = jnp.where(_lane((taps * L, 2 * L)) == L + (shift_row & (L - 1)) - 1 - shift_row // L, 1.0, 0.0).astype(BF16)

    def conv_and_gates(c, carry):
        r0 = pl.multiple_of(c * L, L)
        cur = a_ref[rows_of(c), :2 * gw]
        before = a_ref[pl.ds(pl.multiple_of(jnp.maximum(r0 - L, 0), L), L), :2 * gw]
        before = before * jnp.where(r0 > 0, 1.0, 0.0).astype(BF16)
        shifted = _nn(shift_mats, jnp.concatenate([before, cur], axis=0))
        acc = cur.astype(F32) * conv_ref[CONV_WIDTH_A - 1:CONV_WIDTH_A, :]
        for j in range(1, CONV_WIDTH_A):
            acc = acc + shifted[(j - 1) * L:j * L, :] * conv_ref[CONV_WIDTH_A - 1 - j:CONV_WIDTH_A - j, :]
        acc = _silu(acc)
        q_s[rows_of(c), :] = acc[:, :gw].astype(BF16)
        k_s[rows_of(c), :] = (acc[:, gw:] * (HEAD_DIM ** -0.5)).astype(BF16)
        gates = g_ref[rows_of(c), :]
        bcum = _cumsum_rows(_log_sigmoid(gates), L)
        bc_s[rows_of(c), :] = bcum
        gt_s[c] = jnp.transpose(gates)
        bt_s[c] = jnp.transpose(bcum)
        return carry

    ones_bd = _block_ones(gw)
    ones_rows = jnp.where((_row((N_HEADS * L, gw)) // L) == (_lane((N_HEADS * L, gw)) >> 6), 1.0, 0.0).astype(BF16)
    tril = _lane((L, L)) <= _row((L, L))
    bd_mask = (_row((gw, gw)) >> 6) == (_lane((gw, gw)) >> 6)
    rows8 = lambda u: jnp.broadcast_to(u, (SUBLANES, u.shape[-1]))

    def local_state(i, carry):
        cs = [i * grp + j for j in range(grp)]
        for c in cs:
            conv_and_gates(c, carry)
        ks = [k_s[rows_of(c), :] for c in cs]
        e_fulls, ves = [], []
        for c in cs:
            v = a_ref[rows_of(c), 2 * gw:3 * gw]
            ig = pltpu.roll(g_ref[rows_of(c), :], shift=GATE_F - GATE_I, axis=1)
            w_loc = bc_s[pl.ds(pl.multiple_of(c * L, L) + L - 1, 1), :] - bc_s[rows_of(c), :] + ig
            m_loc = jnp.max(w_loc, axis=0, keepdims=True)
            ml_s[c] = rows8(m_loc)
            e_full = _per_head(_head_cols(jnp.exp(w_loc - m_loc), GATE_F), (L, gw))
            e_fulls.append(e_full)
            ves.append((v.astype(F32) * e_full).astype(BF16))
        c_locs = [_tn(ve, k) for ve, k in zip(ves, ks)]
        for c, c_loc, e_full, k in zip(cs, c_locs, e_fulls, ks):
            cl_s[c] = jnp.where(bd_mask, c_loc, 0.0)
            nl_s[c] = rows8(jnp.sum(e_full * k.astype(F32), axis=0, keepdims=True))
        return carry

    lax.fori_loop(0, nc // grp, local_state, 0)

    def scan(c, carry):
        c_prev, n_prev, m_prev = carry
        g_tot = bc_s[pl.ds(pl.multiple_of(c * L, L) + L - 1, 1), :]
        m_loc = ml_s[c][0:1, :]
        m_new = jnp.maximum(g_tot + m_prev, m_loc)
        a_row = jnp.exp(g_tot + m_prev - m_new)
        b_row = jnp.exp(m_loc - m_new)
        a_h, b_h = _head_cols(a_row, GATE_F), _head_cols(b_row, GATE_F)
        c_loc = cl_s[c]
        head_rows = lambda u, h: u[h * HEAD_DIM:(h + 1) * HEAD_DIM, :]
        c_new = jnp.concatenate([a_h[h] * head_rows(c_prev, h) + b_h[h] * head_rows(c_loc, h) for h in range(N_HEADS)], axis=0)
        n_new = _per_head(a_h, (1, gw)) * n_prev + _per_head(b_h, (1, gw)) * nl_s[c][0:1, :]
        return c_new, n_new, m_new

    def outputs(i, carry):
        cs = [i * grp + j for j in range(grp)]
        states = [carry]
        for c in cs:
            states.append(scan(c, states[-1]))
        hid = _lane((L, gw)) >> 6
        qs = [q_s[rows_of(c), :] for c in cs]
        ks = [k_s[rows_of(c), :] for c in cs]
        vs = [a_ref[rows_of(c), 2 * gw:3 * gw] for c in cs]
        logits = [[_nt(jnp.where(hid == h, q, jnp.zeros_like(q)), k) for h in range(N_HEADS)] for q, k in zip(qs, ks)]
        num_inter = [_nt(q, st[0].astype(BF16)) for q, st in zip(qs, states)]
        den_inter = [_group_sum(q.astype(F32) * st[1], ones_bd) for q, st in zip(qs, states)]
        shape = (L, gw)
        s_cat, iws, floors = [], [], []
        for j, c in enumerate(cs):
            b_col = bc_s[rows_of(c), :]
            g_row = gt_s[c]
            b_row = bt_s[c]
            ig = pltpu.roll(g_ref[rows_of(c), :], shift=GATE_F - GATE_I, axis=1)
            m_inter = b_col + states[j][2]
            m_t = jnp.maximum(m_inter, b_col + _cummax_rows(ig - b_col, L))
            iws.append(_per_head(_head_cols(jnp.exp(m_inter - m_t), GATE_F), shape))
            floors.append(_per_head(_head_cols(jnp.exp(-m_t), GATE_F), shape))
            col_arg = b_col - m_t
            s_list = []
            for h in range(N_HEADS):
                row_arg = g_row[GATE_I + h:GATE_I + h + 1, :] - b_row[GATE_F + h:GATE_F + h + 1, :]
                arg = jnp.where(tril, col_arg[:, GATE_F + h:GATE_F + h + 1] + row_arg, -jnp.inf)
                s_list.append((logits[j][h] * jnp.exp(arg)).astype(BF16))
            s_cat.append(jnp.concatenate(s_list, axis=1))
        num_intra = [_nn(s, _stack_heads(v)) for s, v in zip(s_cat, vs)]
        den_intra = [_nn(s, ones_rows) for s in s_cat]
        hhs = []
        for j in range(grp):
            num = num_intra[j] + iws[j] * num_inter[j]
            den = den_intra[j] + iws[j] * den_inter[j]
            hhs.append(num / jnp.maximum(jnp.abs(den), floors[j]))
        mss = [_group_sum(hh * hh, ones_bd) for hh in hhs]
        for c, hh, ms in zip(cs, hhs, mss):
            z = a_ref[rows_of(c), 3 * gw:4 * gw].astype(F32)
            y = hh * lax.rsqrt(ms * (1.0 / HEAD_DIM) + NORM_EPS) * ng_ref[...] * _silu(z)
            o_ref[rows_of(c), :] = y.astype(o_ref.dtype)
        return states[-1]

    lax.fori_loop(0, nc // grp, outputs,
                  (jnp.zeros((gw, gw), F32), jnp.zeros((1, gw), F32), jnp.zeros((1, LANES), F32)))


def _mlstm(a, gates, conv, ng, layer):
    b, s, _ = a.shape
    gw = GROUP_WIDTH
    L = MLSTM_CHUNK
    nc = s // L
    assert nc % MLSTM_GROUP == 0
    return _layered_call(
        _mlstm_kernel,
        grid=(b,),
        in_specs=[_row_spec((None, s, SEG_A)), _row_spec((None, s, SEG_G)),
                  _layer_spec((CONV_WIDTH_A, 2 * gw)), _layer_spec((1, gw))],
        out_specs=_row_spec((None, s, gw)),
        out_shape=jax.ShapeDtypeStruct((b, s, gw), BF16),
        scratch_shapes=[
            pltpu.VMEM((s, gw), BF16), pltpu.VMEM((s, gw), BF16),
            pltpu.VMEM((s, SEG_G), F32),
            pltpu.VMEM((nc, SEG_G, L), F32), pltpu.VMEM((nc, SEG_G, L), F32),
            pltpu.VMEM((nc, gw, gw), F32), pltpu.VMEM((nc, SUBLANES, gw), F32), pltpu.VMEM((nc, SUBLANES, LANES), F32),
        ],
        name="mlstm",
    )(layer, a, gates, conv, ng)


def _pool_kernel(b_ref, w_ref, sc_ref, o_ref):
    gw = GROUP_WIDTH
    u = b_ref[:, :gw].astype(F32)
    z = b_ref[:, gw:].astype(F32)
    head = POOL_WINDOWS[-1]
    grp = _lane((u.shape[0] + head, gw)) >> 6
    lane_grp = _lane((1, gw)) >> 6
    run, width = jnp.concatenate([jnp.zeros((head, gw), F32), u], axis=0), 1
    window = jnp.full((1, gw), float(POOL_WINDOWS[-1]), F32)
    sums = []
    for win in POOL_WINDOWS:
        while width < win:
            run = run + pltpu.roll(run, shift=width, axis=0)
            width *= 2
        sums.append(run)
    window_sum = sums[-1]
    for gi in range(len(POOL_WINDOWS) - 2, -1, -1):
        window_sum = jnp.where(grp == gi, sums[gi], window_sum)
        window = jnp.where(lane_grp == gi, float(POOL_WINDOWS[gi]), window)
    window_sum = window_sum[head:, :]
    pos = _row((head, gw)).astype(F32)
    first = window_sum[:head, :] / jnp.minimum(pos + 1.0, window) - u[:head, :]
    rest = window_sum[head:, :] * (1.0 / window) - u[head:, :]
    pooled = jnp.concatenate([first, rest], axis=0)
    mixed = _nn(pooled.astype(BF16), w_ref[...])
    o_ref[...] = (mixed * sc_ref[...] * _silu(z)).astype(o_ref.dtype)


def _pool(bx, w_bd, scale, layer):
    b, s, _ = bx.shape
    gw = GROUP_WIDTH
    return _layered_call(
        _pool_kernel,
        grid=(b,),
        in_specs=[_row_spec((None, s, SEG_B)), _layer_spec((gw, gw)), _layer_spec((1, gw))],
        out_specs=_row_spec((None, s, gw)),
        out_shape=jax.ShapeDtypeStruct((b, s, gw), BF16),
        name="pool",
    )(layer, bx, w_bd, scale)


def _cat_mm(xs, ys):
    return [_nn(x.astype(BF16), _stack_heads(y.astype(BF16))) for x, y in zip(xs, ys)]


def _unit_lower_inverse(mats):
    t = RWKV_CHUNK
    shift = RWKV_SUB.bit_length() - 1
    shape = mats[0].shape
    rb = _row(shape) >> shift
    col = _lane(shape) & (t - 1)
    cb = col >> shift
    eye = jnp.where(col == _row(shape), 1.0, 0.0)
    add = lambda xs, ys: [x + y for x, y in zip(xs, ys)]
    d = [jnp.where(rb == cb, a, 0.0) for a in mats]
    e = [jnp.where(rb == cb, 0.0, a) for a in mats]
    d2 = _cat_mm(d, d)
    d4 = _cat_mm(d2, d2)
    x = [eye + di for di in d]
    x = add(x, _cat_mm(x, d2))
    d8 = _cat_mm(d4, d4)
    x = add(x, _cat_mm(x, d4))
    dinv = add(x, _cat_mm(x, d8))
    f = _cat_mm(dinv, e)
    f2 = _cat_mm(f, f)
    w = [eye + fi for fi in f]
    w = add(w, _cat_mm(w, f2))
    return _cat_mm(w, dinv)


def _rwkv_kernel(c_ref, p_ref, w2a2_ref, o_ref,
                 v_s, bonus_s, pr_s, qk_s, ul_s, ar_s, ge_s, ub_s, rs_s, yc_s):
    n_seq, s_len = c_ref.shape[0], c_ref.shape[1]
    gw = GROUP_WIDTH
    T = RWKV_CHUNK
    grp = RWKV_GROUP
    rows = T * grp
    nc = s_len // T
    prm = lambda i: p_ref[i:i + 1, :]

    ones_bd = _block_ones(gw)
    col = _lane((T, N_HEADS * T)) & (T - 1)
    row = _row((T, N_HEADS * T))
    strict = col < row
    incl = col <= row
    bd_mask = (_row((gw, gw)) >> 6) == (_lane((gw, gw)) >> 6)
    first_row = _row((rows, gw)) == 0

    def prepare(e, i, carry):
        r0 = pl.multiple_of(i * rows, rows)
        tile = pl.ds(r0, rows)
        halo = c_ref[e, pl.ds(pl.multiple_of(jnp.maximum(r0 - BF16_ROWS, 0), BF16_ROWS), BF16_ROWS), :]
        prev = halo[BF16_ROWS - 1:BF16_ROWS, :].astype(F32) * jnp.where(r0 > 0, 1.0, 0.0)

        def shifted(lo, hi, mu):
            u = c_ref[e, tile, lo:hi].astype(F32)
            before = jnp.where(first_row[:, :hi - lo], prev[:, lo:hi], pltpu.roll(u, shift=1, axis=0))
            return u + mu * (before - u)

        wa = shifted(4 * gw, 4 * gw + 2 * LORA, p_ref[_P_MU_WA:_P_MU_WA + 1, :2 * LORA])
        wa = jnp.where(_lane(wa.shape) < LORA, jnp.tanh(wa), wa)
        lora = _nn(wa.astype(BF16), w2a2_ref[...])
        ld = -DECAY_SCALE * _sigmoid(prm(_P_W0) + lora[:, :gw])
        a = _sigmoid(prm(_P_A0) + lora[:, gw:])
        r = shifted(0, gw, prm(_P_MU_R))
        k = shifted(gw, 2 * gw, prm(_P_MU_K))
        v = shifted(2 * gw, 3 * gw, prm(_P_MU_V))
        kk = k * prm(_P_KEY_K)
        kk = kk / jnp.maximum(jnp.sqrt(_group_sum(kk * kk, ones_bd)), 1e-12)
        k = k * (1.0 + (a - 1.0) * prm(_P_KEY_A))
        bonus_s[e, tile, :] = (_group_sum(r * k * prm(_P_BONUS), ones_bd) * v).astype(BF16)
        vb = v.astype(BF16)
        v_s[e, tile, :] = vb

        lc = _cumsum_rows(ld, T)
        g_inc = jnp.exp(lc)
        g_inv = jnp.exp(-lc)
        pt = (-kk * jnp.exp(lc - ld)).astype(BF16)
        rt = (r * g_inc).astype(BF16)
        qt = (kk * a * g_inv).astype(BF16)
        kt = (k * g_inv).astype(BF16)

        part = lambda u, j: u[j * T:(j + 1) * T, :]
        chunks = range(grp)
        prs = [jnp.concatenate([part(pt, j), part(rt, j)], axis=0) for j in chunks]
        gs = [_nt(prs[j], jnp.concatenate([_stack_heads(part(qt, j)), _stack_heads(part(kt, j))], axis=0))
              for j in chunks]
        a_pq = [jnp.where(strict, g[:T, :N_HEADS * T], 0.0) for g in gs]
        a_pk = [jnp.where(strict, g[:T, N_HEADS * T:], 0.0).astype(BF16) for g in gs]
        apv = [_nn(a_pk[j], _stack_heads(part(vb, j))) for j in chunks]
        tinv = [t.astype(BF16) for t in _unit_lower_inverse(a_pq)]
        p_fold = [_nn(tinv[j], _stack_heads(part(pt, j))) for j in chunks]
        u_loc = [_nn(tinv[j], _stack_heads(apv[j].astype(BF16))) for j in chunks]
        for j in chunks:
            c = i * grp + j
            pr_s[e, c] = jnp.concatenate([p_fold[j].astype(BF16), part(rt, j)], axis=0)
            qk_s[e, c] = jnp.concatenate([part(qt, j), part(kt, j)], axis=0)
            ul_s[e, c] = u_loc[j].astype(BF16)
            ar_s[e, c] = jnp.concatenate([jnp.where(incl, gs[j][T:, :N_HEADS * T], 0.0),
                                          jnp.where(incl, gs[j][T:, N_HEADS * T:], 0.0)], axis=1).astype(BF16)
            ge_s[e, c] = jnp.broadcast_to(part(g_inc, j)[T - 1:T, :], (SUBLANES, gw))
        return carry

    for e in range(n_seq):
        lax.fori_loop(0, nc // grp, functools.partial(prepare, e), 0)

    seqs = range(n_seq)

    rows_of = lambda c: pl.ds(pl.multiple_of(c * T, T), T)

    def project(c, states):
        return [_nt(pr_s[e, c], states[e].astype(BF16)) for e in seqs]

    def correct(c, prs):
        ubs = [(prs[e][:T] + ul_s[e, c].astype(F32)).astype(BF16) for e in seqs]
        dss = [_tn(jnp.concatenate([ubs[e], v_s[e, rows_of(c), :]], axis=0), qk_s[e, c]) for e in seqs]
        for e in seqs:
            ub_s[e, c] = ubs[e]
            rs_s[e, c] = prs[e][T:].astype(BF16)
        return dss

    def advance(c, states, dss):
        return tuple((states[e] + jnp.where(bd_mask, dss[e], 0.0)) * ge_s[e, c][0:1, :] for e in seqs)

    def readout(c):
        return [rs_s[e, c].astype(F32)
                + _nn(ar_s[e, c], jnp.concatenate([_stack_heads(ub_s[e, c]), _stack_heads(v_s[e, rows_of(c), :])], axis=0))
                for e in seqs]

    def centre(ys):
        means = [_group_sum(y, ones_bd) * (1.0 / HEAD_DIM) for y in ys]
        return [y - m for y, m in zip(ys, means)]

    def variance(ycs):
        return [_group_sum(yc * yc, ones_bd) * (1.0 / HEAD_DIM) for yc in ycs]

    def emit(c, ycs, vrs):
        for e in seqs:
            y = ycs[e] * lax.rsqrt(vrs[e] + GN_EPS) * prm(_P_GN_G) + prm(_P_GN_B) + bonus_s[e, rows_of(c), :].astype(F32)
            z = c_ref[e, rows_of(c), 3 * gw:4 * gw].astype(F32)
            o_ref[e, rows_of(c), :] = (y * _silu(z)).astype(o_ref.dtype)

    def step(c, states):
        prs = project(c, states)
        ys = readout(c - 1)
        older = [yc_s[e, c & 1] for e in seqs]
        vrs = variance(older)
        dss = correct(c, prs)
        ycs = centre(ys)
        new_states = advance(c, states, dss)
        for e in seqs:
            yc_s[e, (c - 1) & 1] = ycs[e]
        emit(c - 2, older, vrs)
        return new_states

    zero_states = tuple(jnp.zeros((gw, gw), F32) for _ in seqs)
    states = advance(0, zero_states, correct(0, project(0, zero_states)))
    first = centre(readout(0))
    states = advance(1, states, correct(1, project(1, states)))
    for e in seqs:
        yc_s[e, 0] = first[e]
    states = lax.fori_loop(2, nc, step, states, unroll=RWKV_STEP_UNROLL)
    last = centre(readout(nc - 1))
    older = [yc_s[e, (nc - 2) & 1] for e in seqs]
    emit(nc - 2, older, variance(older))
    emit(nc - 1, last, variance(last))


def _rwkv(cx, prm, w2a2, layer):
    b, s, _ = cx.shape
    gw = GROUP_WIDTH
    T = RWKV_CHUNK
    nc = s // T
    n_seq = RWKV_PAIR if b % RWKV_PAIR == 0 else 1
    assert nc % RWKV_GROUP == 0
    per_chunk = lambda r, w, dt: pltpu.VMEM((n_seq, nc, r, w), dt)
    return _layered_call(
        _rwkv_kernel,
        grid=(b // n_seq,),
        in_specs=[_row_spec((n_seq, s, SEG_C)), _layer_spec((_P_ROWS, gw)), _layer_spec((2 * LORA, 2 * gw))],
        out_specs=_row_spec((n_seq, s, gw)),
        out_shape=jax.ShapeDtypeStruct((b, s, gw), BF16),
        scratch_shapes=[
            pltpu.VMEM((n_seq, s, gw), BF16), pltpu.VMEM((n_seq, s, gw), BF16),
            per_chunk(2 * T, gw, BF16), per_chunk(2 * T, gw, BF16),
            per_chunk(T, gw, BF16),
            per_chunk(T, 2 * N_HEADS * T, BF16), per_chunk(SUBLANES, gw, F32),
            per_chunk(T, gw, BF16), per_chunk(T, gw, BF16),
            pltpu.VMEM((n_seq, 2, T, gw), F32),
        ],
        name="rwkv",
    )(layer, cx, prm, w2a2)


def _fox_kernel(d_ref, g_ref, qkg_ref, o_ref, q_s, k_s, vh_s, ft_s):
    s_len = d_ref.shape[0]
    gw = GROUP_WIDTH
    tq = min(FOX_QBLOCK, s_len)
    ones_bd = _block_ones(gw)

    def head_norm(u, g):
        ms = _group_sum(u * u, ones_bd) * (1.0 / HEAD_DIM)
        return u * lax.rsqrt(ms + NORM_EPS) * g

    q_s[...] = (head_norm(d_ref[:, 0:gw].astype(F32), qkg_ref[0:1, :]) * (LOG2_E * HEAD_DIM ** -0.5)).astype(BF16)
    k_s[...] = head_norm(d_ref[:, gw:2 * gw].astype(F32), qkg_ref[1:2, :]).astype(BF16)
    v = d_ref[:, 2 * gw:3 * gw]
    den_lane = [((h + 1) % N_HEADS) * HEAD_DIM for h in range(N_HEADS)]
    for h in range(N_HEADS):
        spare = jnp.where(_lane(v.shape) == den_lane[h], 1.0, 0.0).astype(BF16)
        vh_s[h] = jnp.where((_lane(v.shape) >> 6) == h, v, spare)
    ft_s[...] = jnp.transpose(_cumsum_rows(_log_sigmoid(g_ref[...]), s_len) * LOG2_E)

    tril = _lane((tq, tq)) <= _row((tq, tq))
    hid = _lane((tq, gw)) >> 6
    for blk in range(s_len // tq):
        q0, q1 = blk * tq, (blk + 1) * tq
        q = q_s[q0:q1, :]
        out = jnp.zeros((tq, gw), F32)
        for h in range(N_HEADS):
            qh = jnp.where(hid == h, q, jnp.zeros_like(q))
            f_row = ft_s[GATE_D + h:GATE_D + h + 1, :]
            l_diag = jnp.where(tril, _nt(qh, k_s[q0:q1, :]) - f_row[:, q0:q1], -jnp.inf)
            mx = jnp.max(l_diag, axis=-1, keepdims=True)
            if blk:
                l_off = _nt(qh, k_s[0:q0, :]) - f_row[:, 0:q0]
                mx = jnp.maximum(mx, jnp.max(l_off, axis=-1, keepdims=True))
            acc = _nn(jnp.exp2(l_diag - mx).astype(BF16), vh_s[h, q0:q1, :])
            if blk:
                acc = acc + _nn(jnp.exp2(l_off - mx).astype(BF16), vh_s[h, 0:q0, :])
            out = jnp.where(hid == h, acc * (1.0 / acc[:, den_lane[h]:den_lane[h] + 1]), out)
        z = d_ref[q0:q1, 3 * gw:4 * gw].astype(F32)
        o_ref[q0:q1, :] = (out * _silu(z)).astype(o_ref.dtype)


def _fox(dx, gates, qkg, layer):
    b, s, _ = dx.shape
    gw = GROUP_WIDTH
    return _layered_call(
        _fox_kernel,
        grid=(b,),
        in_specs=[_row_spec((None, s, SEG_D)), _row_spec((None, s, SEG_G)), _layer_spec((2, gw))],
        out_specs=_row_spec((None, s, gw)),
        out_shape=jax.ShapeDtypeStruct((b, s, gw), BF16),
        scratch_shapes=[pltpu.VMEM((s, gw), BF16), pltpu.VMEM((s, gw), BF16),
                        pltpu.VMEM((N_HEADS, s, gw), BF16), pltpu.VMEM((SEG_G, s), F32)],
        name="fox",
    )(layer, dx, gates, qkg)


def _mem_kv_kernel(mem_ref, g_ref, w_ref, o_ref):
    b, n_mem, d = mem_ref.shape
    mn = _rms_rows(mem_ref[...].reshape(b * n_mem, d), g_ref[...]).astype(BF16)
    o_ref[...] = _nn(mn, w_ref[...]).astype(o_ref.dtype).reshape(o_ref.shape)


def _mem_kv(mem, g, wkv):
    b, n_mem, d = mem.shape
    depth = wkv.shape[0]
    return pl.pallas_call(
        _mem_kv_kernel,
        grid=(depth,),
        in_specs=[
            pl.BlockSpec((b, n_mem, d), lambda l: (0, 0, 0)),
            pl.BlockSpec((1, d), lambda l: (0, 0)),
            pl.BlockSpec((None, d, 2 * d), lambda l: (l, 0, 0)),
        ],
        out_specs=pl.BlockSpec((None, b, n_mem, 2 * d), lambda l: (l, 0, 0, 0)),
        out_shape=jax.ShapeDtypeStruct((depth, b, n_mem, 2 * d), BF16),
        compiler_params=_params(),
        name="mem_kv",
    )(mem, g, wkv)


def _out_xattn_kernel(x_ref, ya_ref, yb_ref, yc_ref, yd_ref, wout_ref, g_ref, wq_ref, kv_ref, wo_ref,
                      o_ref, *, n_mem_heads):
    d = x_ref.shape[-1]
    dh = d // n_mem_heads
    tm = x_ref.shape[0]
    gain = lambda i: g_ref[i:i + 1, :]
    parts = [pl.ds(j * (tm // XATTN_PARTS), tm // XATTN_PARTS) for j in range(XATTN_PARTS)]
    ys = [jnp.concatenate([ya_ref[r, :], yb_ref[r, :], yc_ref[r, :], yd_ref[r, :]], axis=-1) for r in parts]
    mixed = [_nn(y, wout_ref[...]) for y in ys]
    xs = [x_ref[r, :] + _rms_rows(m, gain(_G_POST)) for r, m in zip(parts, mixed)]
    hms = [_rms_rows(x, gain(_G_XPRE)).astype(BF16) for x in xs]
    qs = [_nn(hm, wq_ref[...]).astype(BF16) for hm in hms]
    outs = [[] for _ in parts]
    for h in range(n_mem_heads):
        kh = kv_ref[:, h * dh:(h + 1) * dh]
        vh = kv_ref[:, d + h * dh:d + (h + 1) * dh]
        logits = [_nt(q[:, h * dh:(h + 1) * dh], kh) * (dh ** -0.5) for q in qs]
        ps = [jnp.exp(l - jnp.max(l, axis=-1, keepdims=True)) for l in logits]
        ps = [(p / jnp.sum(p, axis=-1, keepdims=True)).astype(BF16) for p in ps]
        for j, p in enumerate(ps):
            outs[j].append(_nn(p, vh).astype(BF16))
    attn = [_nn(jnp.concatenate(o, axis=-1), wo_ref[...]) for o in outs]
    for r, x, o in zip(parts, xs, attn):
        o_ref[r, :] = x + _rms_rows(o, gain(_G_XPOST))


def _out_xattn(x2, ys, wout, gains, wq, kv, wo, seq_len, layer, n_mem_heads=4):
    m, d = x2.shape
    gw = GROUP_WIDTH
    tm = min(XATTN_ROW_TILE, seq_len)
    per_seq = seq_len // tm
    n_mem = kv.shape[2]
    return _layered_call(
        functools.partial(_out_xattn_kernel, n_mem_heads=n_mem_heads),
        grid=(m // tm,),
        in_specs=[_row_spec((tm, d))] + [_row_spec((tm, gw))] * 4 + [
            _layer_spec((4 * gw, d)),
            _layer_spec((_G_ROWS, d)),
            _layer_spec((d, d)),
            pl.BlockSpec((None, None, n_mem, 2 * d), lambda i, layer: (layer[0], i // per_seq, 0, 0)),
            _layer_spec((d, d)),
        ],
        out_specs=_row_spec((tm, d)),
        out_shape=jax.ShapeDtypeStruct((m, d), F32),
        name="out_xattn",
    )(layer, x2, *ys, wout, gains, wq, kv, wo)


_IN_SIZES = (("a_q", GROUP_WIDTH), ("a_k", GROUP_WIDTH), ("a_v", GROUP_WIDTH), ("a_i", N_HEADS), ("a_f", N_HEADS),
             ("a_z", GROUP_WIDTH), ("b_x", GROUP_WIDTH), ("b_z", GROUP_WIDTH),
             ("c_r", GROUP_WIDTH), ("c_k", GROUP_WIDTH), ("c_v", GROUP_WIDTH), ("c_w", LORA), ("c_a", LORA),
             ("c_z", GROUP_WIDTH), ("d_q", GROUP_WIDTH), ("d_k", GROUP_WIDTH), ("d_v", GROUP_WIDTH),
             ("d_f", N_HEADS), ("d_z", GROUP_WIDTH))
_IN_ORDER = ("a_q", "a_k", "a_v", "a_z", "b_x", "b_z", "c_r", "c_k", "c_v", "c_z", "c_w", "c_a",
             "d_q", "d_k", "d_v", "d_z", "a_i", "a_f", "d_f")
N_IN = sum(size for _, size in _IN_SIZES)


def _relayout_in(w, axis):
    pieces, off = {}, 0
    for name, size in _IN_SIZES:
        pieces[name] = lax.slice_in_dim(w, off, off + size, axis=axis)
        off += size
    pad_shape = list(w.shape)
    pad_shape[axis] = SEG_G - 3 * N_HEADS
    return jnp.concatenate([pieces[n] for n in _IN_ORDER] + [jnp.zeros(pad_shape, w.dtype)], axis=axis)


def _relayout_w_kernel(w_ref, o_ref):
    src = {}
    off = 0
    for name, size in _IN_SIZES:
        src[name] = (off, size)
        off += size
    dst = 0
    gate_rows = []
    for name in _IN_ORDER:
        start, size = src[name]
        if size < BF16_ROWS:
            gate_rows.append(w_ref[start:start + size, :])
        else:
            o_ref[dst:dst + size, :] = w_ref[start:start + size, :].astype(o_ref.dtype)
            dst += size
    used = sum(g.shape[0] for g in gate_rows)
    gate_rows.append(jnp.zeros((o_ref.shape[0] - dst - used, o_ref.shape[1]), F32))
    o_ref[dst:, :] = jnp.concatenate(gate_rows, axis=0).astype(o_ref.dtype)


def _relayout_w(w_t):
    depth, n_in, d = w_t.shape
    assert n_in == N_IN
    tc = RELAYOUT_COLS
    return pl.pallas_call(
        _relayout_w_kernel,
        grid=(depth, d // tc),
        in_specs=[pl.BlockSpec((None, n_in, tc), lambda l, i: (l, 0, i))],
        out_specs=pl.BlockSpec((None, N_PROJ, tc), lambda l, i: (l, 0, i)),
        out_shape=jax.ShapeDtypeStruct((depth, N_PROJ, d), BF16),
        compiler_params=_params(),
        name="relayout_w",
    )(w_t)


def _block_diag(blocks):
    n, k, _ = blocks.shape
    eye = jnp.eye(n, dtype=blocks.dtype)
    return jnp.einsum("gcd,gh->gchd", blocks, eye).reshape(n * k, n * k)


def kernel(x, mem, pre_norm_g, post_norm_g, w_in, b_in, conv_a, norm_a_g, pool_w, pool_scale, shift_mu_c,
           decay_w0, decay_w2, iclr_a0, iclr_a2, key_k, key_a, bonus_u, gn_c_g, gn_c_b, qk_norm_d, w_out,
           mem_norm_g, xattn_pre_g, xattn_post_g, xattn_wq, xattn_wkv, xattn_wo):
    bsz, seq, d = x.shape
    depth = w_in.shape[0]
    gw = GROUP_WIDTH

    w_in_r = _relayout_w(jnp.swapaxes(w_in, 1, 2))
    b_in_r = _relayout_in(b_in, 1).reshape(depth, 1, N_PROJ)
    gains = jnp.stack([pre_norm_g, post_norm_g, xattn_pre_g, xattn_post_g]
                      + [jnp.zeros_like(pre_norm_g)] * (_G_ROWS - 4), axis=1)
    mu = shift_mu_c
    zeros_row = jnp.zeros((depth, gw), F32)
    mu_wa = jnp.concatenate([mu[:, 3 * gw:], jnp.zeros((depth, gw - 2 * LORA), F32)], axis=-1)
    rows = [mu[:, 0:gw], mu[:, gw:2 * gw], mu[:, 2 * gw:3 * gw], decay_w0, iclr_a0, key_k, key_a, bonus_u,
            gn_c_g, gn_c_b, mu_wa] + [zeros_row] * (_P_ROWS - 11)
    rwkv_prm = jnp.stack(rows, axis=1)
    zl = jnp.zeros((depth, LORA, gw), F32)
    w2a2 = jnp.concatenate([jnp.concatenate([decay_w2, zl], axis=-1),
                            jnp.concatenate([zl, iclr_a2], axis=-1)], axis=1).astype(BF16)
    pool_bd = jax.vmap(_block_diag)(pool_w).astype(BF16)
    qkg = jnp.tile(qk_norm_d, (1, 1, N_HEADS))
    norm_a = norm_a_g.reshape(depth, 1, gw)
    pool_sc = pool_scale.reshape(depth, 1, gw)
    w_out_b = w_out.astype(BF16)
    wq_b = xattn_wq.astype(BF16)
    wo_b = xattn_wo.astype(BF16)

    kv_all = _mem_kv(mem, mem_norm_g.reshape(1, d), xattn_wkv.astype(BF16))

    x2 = x.reshape(bsz * seq, d)
    sh = lambda u: u.reshape(bsz, seq, u.shape[-1])
    for l in range(depth):
        layer = jnp.full((1,), l, jnp.int32)
        pa, pb, pc, pd, pg = _in_proj(x2, gains, w_in_r, b_in_r, layer)
        gates = sh(pg)
        y_a = _mlstm(sh(pa), gates, conv_a, norm_a, layer)
        y_b = _pool(sh(pb), pool_bd, pool_sc, layer)
        y_c = _rwkv(sh(pc), rwkv_prm, w2a2, layer)
        y_d = _fox(sh(pd), gates, qkg, layer)
        ys = [u.reshape(bsz * seq, gw) for u in (y_a, y_b, y_c, y_d)]
        x2 = _out_xattn(x2, ys, w_out_b, gains, wq_b, kv_all, wo_b, seq, layer)
    return x2.reshape(bsz, seq, d)
```

```python
import functools

import jax
import jax.numpy as jnp
from jax import lax
from jax.experimental import pallas as pl
from jax.experimental.pallas import tpu as pltpu

F32 = jnp.float32
BF16 = jnp.bfloat16

HEAD_DIM = 64
N_HEADS = 4
GROUP_WIDTH = HEAD_DIM * N_HEADS
LANES = 128
SUBLANES = 8
BF16_ROWS = 16
NORM_EPS = 1e-6
GN_EPS = 64e-5
LOG2_E = 1.4426950408889634
POOL_WINDOWS = (2, 4, 8, 16)
CONV_WIDTH_A = 4
LORA = 64
MLSTM_CHUNK = 128
MLSTM_GROUP = 4
RWKV_CHUNK = 64
RWKV_SUB = 16
RWKV_GROUP = 8
RWKV_STEP_UNROLL = 5
RWKV_PAIR = 2
DECAY_SCALE = 0.6065306597126334
FOX_QBLOCK = 256
ROW_TILE = 1024
XATTN_ROW_TILE = 1024
XATTN_PARTS = 4
RELAYOUT_COLS = 256
VMEM_LIMIT = 56 * 1024 * 1024

SEG_A = 4 * GROUP_WIDTH
SEG_B = 2 * GROUP_WIDTH
SEG_C = 4 * GROUP_WIDTH + 2 * LORA
SEG_D = 4 * GROUP_WIDTH
SEG_G = LANES
N_PROJ = SEG_A + SEG_B + SEG_C + SEG_D + SEG_G
GATE_I, GATE_F, GATE_D = 0, N_HEADS, 2 * N_HEADS

(_G_PRE, _G_POST, _G_XPRE, _G_XPOST) = range(4)
_G_ROWS = 8

(_P_MU_R, _P_MU_K, _P_MU_V, _P_W0, _P_A0, _P_KEY_K, _P_KEY_A, _P_BONUS, _P_GN_G, _P_GN_B, _P_MU_WA) = range(11)
_P_ROWS = 16


def _lane(shape):
    return lax.broadcasted_iota(jnp.int32, shape, len(shape) - 1)


def _row(shape):
    return lax.broadcasted_iota(jnp.int32, shape, len(shape) - 2)


def _nt(a, b):
    return lax.dot_general(a, b, (((1,), (1,)), ((), ())), preferred_element_type=F32)


def _tn(a, b):
    return lax.dot_general(a, b, (((0,), (0,)), ((), ())), preferred_element_type=F32)


def _nn(a, b):
    return jnp.dot(a, b, preferred_element_type=F32)


def _sigmoid(x):
    return 1.0 / (1.0 + jnp.exp(-x))


def _silu(x):
    return x * _sigmoid(x)


def _softplus(x):
    return jnp.maximum(x, 0.0) + jnp.log(1.0 + jnp.exp(-jnp.abs(x)))


def _log_sigmoid(x):
    return -_softplus(-x)


def _shift_rows(x, s):
    rolled = pltpu.roll(x, shift=s, axis=0)
    return jnp.where(_row(x.shape) >= s, rolled, 0.0)


def _cumsum_rows(x, seg):
    pos = _row(x.shape) & (seg - 1)
    s = 1
    while s < seg:
        x = x + jnp.where(pos >= s, pltpu.roll(x, shift=s, axis=0), 0.0)
        s *= 2
    return x


def _cummax_rows(x, seg):
    pos = _row(x.shape) & (seg - 1)
    s = 1
    while s < seg:
        x = jnp.maximum(x, jnp.where(pos >= s, pltpu.roll(x, shift=s, axis=0), -jnp.inf))
        s *= 2
    return x


def _stack_heads(x):
    hid = _lane(x.shape) >> 6
    zero = jnp.zeros_like(x)
    return jnp.concatenate([jnp.where(hid == h, x, zero) for h in range(N_HEADS)], axis=0)


def _per_head(cols, shape):
    hid = _lane(shape) >> 6
    out = jnp.broadcast_to(cols[N_HEADS - 1], shape)
    for h in range(N_HEADS - 2, -1, -1):
        out = jnp.where(hid == h, cols[h], out)
    return out


def _head_cols(x, first):
    return [x[:, first + h:first + h + 1] for h in range(N_HEADS)]


def _block_ones(n):
    return jnp.where((_row((n, n)) >> 6) == (_lane((n, n)) >> 6), 1.0, 0.0).astype(BF16)


def _group_sum(x, ones_bd):
    hi = x.astype(BF16)
    lo = (x - hi.astype(F32)).astype(BF16)
    return _nn(hi, ones_bd) + _nn(lo, ones_bd)


def _rms_rows(x, g):
    return x * lax.rsqrt(jnp.mean(x * x, axis=-1, keepdims=True) + NORM_EPS) * g


def _params(**kw):
    return pltpu.CompilerParams(vmem_limit_bytes=VMEM_LIMIT, **kw)


def _layer_spec(shape):
    zeros = (0,) * len(shape)
    return pl.BlockSpec((None,) + tuple(shape), lambda i, layer: (layer[0],) + zeros)


def _row_spec(shape):
    zeros = (0,) * (len(shape) - 1)
    return pl.BlockSpec(tuple(shape), lambda i, layer: (i,) + zeros)


def _layered_call(body, grid, in_specs, out_specs, out_shape, scratch_shapes=(), name=None):
    def with_layer(layer_ref, *refs):
        del layer_ref
        body(*refs)

    return pl.pallas_call(
        with_layer,
        grid_spec=pltpu.PrefetchScalarGridSpec(num_scalar_prefetch=1, grid=grid, in_specs=in_specs,
                                               out_specs=out_specs, scratch_shapes=scratch_shapes),
        out_shape=out_shape, compiler_params=_params(), name=name)


def _in_proj_kernel(x_ref, g_ref, w_ref, b_ref, oa_ref, ob_ref, oc_ref, od_ref, og_ref):
    h = _rms_rows(x_ref[...], g_ref[_G_PRE:_G_PRE + 1, :]).astype(BF16)
    off = 0
    for o_ref, width in ((oa_ref, SEG_A), (ob_ref, SEG_B), (oc_ref, SEG_C), (od_ref, SEG_D), (og_ref, SEG_G)):
        acc = _nt(h, w_ref[off:off + width, :]) + b_ref[:, off:off + width]
        o_ref[...] = acc.astype(o_ref.dtype)
        off += width


def _in_proj(x2, gains, w, b, layer):
    m, d = x2.shape
    tm = min(ROW_TILE, m)
    widths = (SEG_A, SEG_B, SEG_C, SEG_D, SEG_G)
    dtypes = (BF16, BF16, BF16, BF16, F32)
    return _layered_call(
        _in_proj_kernel,
        grid=(m // tm,),
        in_specs=[_row_spec((tm, d)), _layer_spec((_G_ROWS, d)), _layer_spec((N_PROJ, d)), _layer_spec((1, N_PROJ))],
        out_specs=[_row_spec((tm, wd)) for wd in widths],
        out_shape=[jax.ShapeDtypeStruct((m, wd), dt) for wd, dt in zip(widths, dtypes)],
        name="in_proj",
    )(layer, x2, gains, w, b)


def _mlstm_kernel(a_ref, g_ref, conv_ref, ng_ref, o_ref,
                  q_s, k_s, bc_s, gt_s, bt_s, cl_s, nl_s, ml_s):
    s_len = a_ref.shape[0]
    L = MLSTM_CHUNK
    nc = s_len // L
    grp = MLSTM_GROUP
    gw = GROUP_WIDTH

    rows_of = lambda c: pl.ds(pl.multiple_of(c * L, L), L)
    taps = CONV_WIDTH_A - 1
    shift_row = _row((taps * L, 2 * L))
    shift_mats = jnp.where(_lane((taps * L, 2 * L)) == L + (shift_row & (L - 1)) - 1 - shift_row // L, 1.0, 0.0).astype(BF16)

    def conv_and_gates(c, carry):
        r0 = pl.multiple_of(c * L, L)
        cur = a_ref[rows_of(c), :2 * gw]
        before = a_ref[pl.ds(pl.multiple_of(jnp.maximum(r0 - L, 0), L), L), :2 * gw]
        before = before * jnp.where(r0 > 0, 1.0, 0.0).astype(BF16)
        shifted = _nn(shift_mats, jnp.concatenate([before, cur], axis=0))
        acc = cur.astype(F32) * conv_ref[CONV_WIDTH_A - 1:CONV_WIDTH_A, :]
        for j in range(1, CONV_WIDTH_A):
            acc = acc + shifted[(j - 1) * L:j * L, :] * conv_ref[CONV_WIDTH_A - 1 - j:CONV_WIDTH_A - j, :]
        acc = _silu(acc)
        q_s[rows_of(c), :] = acc[:, :gw].astype(BF16)
        k_s[rows_of(c), :] = (acc[:, gw:] * (HEAD_DIM ** -0.5)).astype(BF16)
        gates = g_ref[rows_of(c), :]
        bcum = _cumsum_rows(_log_sigmoid(gates), L)
        bc_s[rows_of(c), :] = bcum
        gt_s[c] = jnp.transpose(gates)
        bt_s[c] = jnp.transpose(bcum)
        return carry

    ones_bd = _block_ones(gw)
    ones_rows = jnp.where((_row((N_HEADS * L, gw)) // L) == (_lane((N_HEADS * L, gw)) >> 6), 1.0, 0.0).astype(BF16)
    tril = _lane((L, L)) <= _row((L, L))
    bd_mask = (_row((gw, gw)) >> 6) == (_lane((gw, gw)) >> 6)
    rows8 = lambda u: jnp.broadcast_to(u, (SUBLANES, u.shape[-1]))

    def local_state(i, carry):
        cs = [i * grp + j for j in range(grp)]
        for c in cs:
            conv_and_gates(c, carry)
        ks = [k_s[rows_of(c), :] for c in cs]
        e_fulls, ves = [], []
        for c in cs:
            v = a_ref[rows_of(c), 2 * gw:3 * gw]
            ig = pltpu.roll(g_ref[rows_of(c), :], shift=GATE_F - GATE_I, axis=1)
            w_loc = bc_s[pl.ds(pl.multiple_of(c * L, L) + L - 1, 1), :] - bc_s[rows_of(c), :] + ig
            m_loc = jnp.max(w_loc, axis=0, keepdims=True)
            ml_s[c] = rows8(m_loc)
            e_full = _per_head(_head_cols(jnp.exp(w_loc - m_loc), GATE_F), (L, gw))
            e_fulls.append(e_full)
            ves.append((v.astype(F32) * e_full).astype(BF16))
        c_locs = [_tn(ve, k) for ve, k in zip(ves, ks)]
        for c, c_loc, e_full, k in zip(cs, c_locs, e_fulls, ks):
            cl_s[c] = jnp.where(bd_mask, c_loc, 0.0)
            nl_s[c] = rows8(jnp.sum(e_full * k.astype(F32), axis=0, keepdims=True))
        return carry

    lax.fori_loop(0, nc // grp, local_state, 0)

    def scan(c, carry):
        c_prev, n_prev, m_prev = carry
        g_tot = bc_s[pl.ds(pl.multiple_of(c * L, L) + L - 1, 1), :]
        m_loc = ml_s[c][0:1, :]
        m_new = jnp.maximum(g_tot + m_prev, m_loc)
        a_row = jnp.exp(g_tot + m_prev - m_new)
        b_row = jnp.exp(m_loc - m_new)
        a_h, b_h = _head_cols(a_row, GATE_F), _head_cols(b_row, GATE_F)
        c_loc = cl_s[c]
        head_rows = lambda u, h: u[h * HEAD_DIM:(h + 1) * HEAD_DIM, :]
        c_new = jnp.concatenate([a_h[h] * head_rows(c_prev, h) + b_h[h] * head_rows(c_loc, h) for h in range(N_HEADS)], axis=0)
        n_new = _per_head(a_h, (1, gw)) * n_prev + _per_head(b_h, (1, gw)) * nl_s[c][0:1, :]
        return c_new, n_new, m_new

    def outputs(i, carry):
        cs = [i * grp + j for j in range(grp)]
        states = [carry]
        for c in cs:
            states.append(scan(c, states[-1]))
        hid = _lane((L, gw)) >> 6
        qs = [q_s[rows_of(c), :] for c in cs]
        ks = [k_s[rows_of(c), :] for c in cs]
        vs = [a_ref[rows_of(c), 2 * gw:3 * gw] for c in cs]
        logits = [[_nt(jnp.where(hid == h, q, jnp.zeros_like(q)), k) for h in range(N_HEADS)] for q, k in zip(qs, ks)]
        num_inter = [_nt(q, st[0].astype(BF16)) for q, st in zip(qs, states)]
        den_inter = [_group_sum(q.astype(F32) * st[1], ones_bd) for q, st in zip(qs, states)]
        shape = (L, gw)
        s_cat, iws, floors = [], [], []
        for j, c in enumerate(cs):
            b_col = bc_s[rows_of(c), :]
            g_row = gt_s[c]
            b_row = bt_s[c]
            ig = pltpu.roll(g_ref[rows_of(c), :], shift=GATE_F - GATE_I, axis=1)
            m_inter = b_col + states[j][2]
            m_t = jnp.maximum(m_inter, b_col + _cummax_rows(ig - b_col, L))
            iws.append(_per_head(_head_cols(jnp.exp(m_inter - m_t), GATE_F), shape))
            floors.append(_per_head(_head_cols(jnp.exp(-m_t), GATE_F), shape))
            col_arg = b_col - m_t
            s_list = []
            for h in range(N_HEADS):
                row_arg = g_row[GATE_I + h:GATE_I + h + 1, :] - b_row[GATE_F + h:GATE_F + h + 1, :]
                arg = jnp.where(tril, col_arg[:, GATE_F + h:GATE_F + h + 1] + row_arg, -jnp.inf)
                s_list.append((logits[j][h] * jnp.exp(arg)).astype(BF16))
            s_cat.append(jnp.concatenate(s_list, axis=1))
        num_intra = [_nn(s, _stack_heads(v)) for s, v in zip(s_cat, vs)]
        den_intra = [_nn(s, ones_rows) for s in s_cat]
        hhs = []
        for j in range(grp):
            num = num_intra[j] + iws[j] * num_inter[j]
            den = den_intra[j] + iws[j] * den_inter[j]
            hhs.append(num / jnp.maximum(jnp.abs(den), floors[j]))
        mss = [_group_sum(hh * hh, ones_bd) for hh in hhs]
        for c, hh, ms in zip(cs, hhs, mss):
            z = a_ref[rows_of(c), 3 * gw:4 * gw].astype(F32)
            y = hh * lax.rsqrt(ms * (1.0 / HEAD_DIM) + NORM_EPS) * ng_ref[...] * _silu(z)
            o_ref[rows_of(c), :] = y.astype(o_ref.dtype)
        return states[-1]

    lax.fori_loop(0, nc // grp, outputs,
                  (jnp.zeros((gw, gw), F32), jnp.zeros((1, gw), F32), jnp.zeros((1, LANES), F32)))


def _mlstm(a, gates, conv, ng, layer):
    b, s, _ = a.shape
    gw = GROUP_WIDTH
    L = MLSTM_CHUNK
    nc = s // L
    assert nc % MLSTM_GROUP == 0
    return _layered_call(
        _mlstm_kernel,
        grid=(b,),
        in_specs=[_row_spec((None, s, SEG_A)), _row_spec((None, s, SEG_G)),
                  _layer_spec((CONV_WIDTH_A, 2 * gw)), _layer_spec((1, gw))],
        out_specs=_row_spec((None, s, gw)),
        out_shape=jax.ShapeDtypeStruct((b, s, gw), BF16),
        scratch_shapes=[
            pltpu.VMEM((s, gw), BF16), pltpu.VMEM((s, gw), BF16),
            pltpu.VMEM((s, SEG_G), F32),
            pltpu.VMEM((nc, SEG_G, L), F32), pltpu.VMEM((nc, SEG_G, L), F32),
            pltpu.VMEM((nc, gw, gw), F32), pltpu.VMEM((nc, SUBLANES, gw), F32), pltpu.VMEM((nc, SUBLANES, LANES), F32),
        ],
        name="mlstm",
    )(layer, a, gates, conv, ng)


def _pool_kernel(b_ref, w_ref, sc_ref, o_ref):
    gw = GROUP_WIDTH
    u = b_ref[:, :gw].astype(F32)
    z = b_ref[:, gw:].astype(F32)
    head = POOL_WINDOWS[-1]
    grp = _lane((u.shape[0] + head, gw)) >> 6
    lane_grp = _lane((1, gw)) >> 6
    run, width = jnp.concatenate([jnp.zeros((head, gw), F32), u], axis=0), 1
    window = jnp.full((1, gw), float(POOL_WINDOWS[-1]), F32)
    sums = []
    for win in POOL_WINDOWS:
        while width < win:
            run = run + pltpu.roll(run, shift=width, axis=0)
            width *= 2
        sums.append(run)
    window_sum = sums[-1]
    for gi in range(len(POOL_WINDOWS) - 2, -1, -1):
        window_sum = jnp.where(grp == gi, sums[gi], window_sum)
        window = jnp.where(lane_grp == gi, float(POOL_WINDOWS[gi]), window)
    window_sum = window_sum[head:, :]
    pos = _row((head, gw)).astype(F32)
    first = window_sum[:head, :] / jnp.minimum(pos + 1.0, window) - u[:head, :]
    rest = window_sum[head:, :] * (1.0 / window) - u[head:, :]
    pooled = jnp.concatenate([first, rest], axis=0)
    mixed = _nn(pooled.astype(BF16), w_ref[...])
    o_ref[...] = (mixed * sc_ref[...] * _silu(z)).astype(o_ref.dtype)


def _pool(bx, w_bd, scale, layer):
    b, s, _ = bx.shape
    gw = GROUP_WIDTH
    return _layered_call(
        _pool_kernel,
        grid=(b,),
        in_specs=[_row_spec((None, s, SEG_B)), _layer_spec((gw, gw)), _layer_spec((1, gw))],
        out_specs=_row_spec((None, s, gw)),
        out_shape=jax.ShapeDtypeStruct((b, s, gw), BF16),
        name="pool",
    )(layer, bx, w_bd, scale)


def _cat_mm(xs, ys):
    return [_nn(x.astype(BF16), _stack_heads(y.astype(BF16))) for x, y in zip(xs, ys)]


def _unit_lower_inverse(mats):
    t = RWKV_CHUNK
    shift = RWKV_SUB.bit_length() - 1
    shape = mats[0].shape
    rb = _row(shape) >> shift
    col = _lane(shape) & (t - 1)
    cb = col >> shift
    eye = jnp.where(col == _row(shape), 1.0, 0.0)
    add = lambda xs, ys: [x + y for x, y in zip(xs, ys)]
    d = [jnp.where(rb == cb, a, 0.0) for a in mats]
    e = [jnp.where(rb == cb, 0.0, a) for a in mats]
    d2 = _cat_mm(d, d)
    d4 = _cat_mm(d2, d2)
    x = [eye + di for di in d]
    x = add(x, _cat_mm(x, d2))
    d8 = _cat_mm(d4, d4)
    x = add(x, _cat_mm(x, d4))
    dinv = add(x, _cat_mm(x, d8))
    f = _cat_mm(dinv, e)
    f2 = _cat_mm(f, f)
    w = [eye + fi for fi in f]
    w = add(w, _cat_mm(w, f2))
    return _cat_mm(w, dinv)


def _rwkv_kernel(c_ref, p_ref, w2a2_ref, o_ref,
                 v_s, bonus_s, pr_s, qk_s, ul_s, ar_s, ge_s, ub_s, rs_s, yc_s):
    n_seq, s_len = c_ref.shape[0], c_ref.shape[1]
    gw = GROUP_WIDTH
    T = RWKV_CHUNK
    grp = RWKV_GROUP
    rows = T * grp
    nc = s_len // T
    prm = lambda i: p_ref[i:i + 1, :]

    ones_bd = _block_ones(gw)
    col = _lane((T, N_HEADS * T)) & (T - 1)
    row = _row((T, N_HEADS * T))
    strict = col < row
    incl = col <= row
    bd_mask = (_row((gw, gw)) >> 6) == (_lane((gw, gw)) >> 6)
    first_row = _row((rows, gw)) == 0

    def prepare(e, i, carry):
        r0 = pl.multiple_of(i * rows, rows)
        tile = pl.ds(r0, rows)
        halo = c_ref[e, pl.ds(pl.multiple_of(jnp.maximum(r0 - BF16_ROWS, 0), BF16_ROWS), BF16_ROWS), :]
        prev = halo[BF16_ROWS - 1:BF16_ROWS, :].astype(F32) * jnp.where(r0 > 0, 1.0, 0.0)

        def shifted(lo, hi, mu):
            u = c_ref[e, tile, lo:hi].astype(F32)
            before = jnp.where(first_row[:, :hi - lo], prev[:, lo:hi], pltpu.roll(u, shift=1, axis=0))
            return u + mu * (before - u)

        wa = shifted(4 * gw, 4 * gw + 2 * LORA, p_ref[_P_MU_WA:_P_MU_WA + 1, :2 * LORA])
        wa = jnp.where(_lane(wa.shape) < LORA, jnp.tanh(wa), wa)
        lora = _nn(wa.astype(BF16), w2a2_ref[...])
        ld = -DECAY_SCALE * _sigmoid(prm(_P_W0) + lora[:, :gw])
        a = _sigmoid(prm(_P_A0) + lora[:, gw:])
        r = shifted(0, gw, prm(_P_MU_R))
        k = shifted(gw, 2 * gw, prm(_P_MU_K))
        v = shifted(2 * gw, 3 * gw, prm(_P_MU_V))
        kk = k * prm(_P_KEY_K)
        kk = kk / jnp.maximum(jnp.sqrt(_group_sum(kk * kk, ones_bd)), 1e-12)
        k = k * (1.0 + (a - 1.0) * prm(_P_KEY_A))
        bonus_s[e, tile, :] = (_group_sum(r * k * prm(_P_BONUS), ones_bd) * v).astype(BF16)
        vb = v.astype(BF16)
        v_s[e, tile, :] = vb

        lc = _cumsum_rows(ld, T)
        g_inc = jnp.exp(lc)
        g_inv = jnp.exp(-lc)
        pt = (-kk * jnp.exp(lc - ld)).astype(BF16)
        rt = (r * g_inc).astype(BF16)
        qt = (kk * a * g_inv).astype(BF16)
        kt = (k * g_inv).astype(BF16)

        part = lambda u, j: u[j * T:(j + 1) * T, :]
        chunks = range(grp)
        prs = [jnp.concatenate([part(pt, j), part(rt, j)], axis=0) for j in chunks]
        gs = [_nt(prs[j], jnp.concatenate([_stack_heads(part(qt, j)), _stack_heads(part(kt, j))], axis=0))
              for j in chunks]
        a_pq = [jnp.where(strict, g[:T, :N_HEADS * T], 0.0) for g in gs]
        a_pk = [jnp.where(strict, g[:T, N_HEADS * T:], 0.0).astype(BF16) for g in gs]
        apv = [_nn(a_pk[j], _stack_heads(part(vb, j))) for j in chunks]
        tinv = [t.astype(BF16) for t in _unit_lower_inverse(a_pq)]
        p_fold = [_nn(tinv[j], _stack_heads(part(pt, j))) for j in chunks]
        u_loc = [_nn(tinv[j], _stack_heads(apv[j].astype(BF16))) for j in chunks]
        for j in chunks:
            c = i * grp + j
            pr_s[e, c] = jnp.concatenate([p_fold[j].astype(BF16), part(rt, j)], axis=0)
            qk_s[e, c] = jnp.concatenate([part(qt, j), part(kt, j)], axis=0)
            ul_s[e, c] = u_loc[j].astype(BF16)
            ar_s[e, c] = jnp.concatenate([jnp.where(incl, gs[j][T:, :N_HEADS * T], 0.0),
                                          jnp.where(incl, gs[j][T:, N_HEADS * T:], 0.0)], axis=1).astype(BF16)
            ge_s[e, c] = jnp.broadcast_to(part(g_inc, j)[T - 1:T, :], (SUBLANES, gw))
        return carry

    for e in range(n_seq):
        lax.fori_loop(0, nc // grp, functools.partial(prepare, e), 0)

    seqs = range(n_seq)

    rows_of = lambda c: pl.ds(pl.multiple_of(c * T, T), T)

    def project(c, states):
        return [_nt(pr_s[e, c], states[e].astype(BF16)) for e in seqs]

    def correct(c, prs):
        ubs = [(prs[e][:T] + ul_s[e, c].astype(F32)).astype(BF16) for e in seqs]
        dss = [_tn(jnp.concatenate([ubs[e], v_s[e, rows_of(c), :]], axis=0), qk_s[e, c]) for e in seqs]
        for e in seqs:
            ub_s[e, c] = ubs[e]
            rs_s[e, c] = prs[e][T:].astype(BF16)
        return dss

    def advance(c, states, dss):
        return tuple((states[e] + jnp.where(bd_mask, dss[e], 0.0)) * ge_s[e, c][0:1, :] for e in seqs)

    def readout(c):
        return [rs_s[e, c].astype(F32)
                + _nn(ar_s[e, c], jnp.concatenate([_stack_heads(ub_s[e, c]), _stack_heads(v_s[e, rows_of(c), :])], axis=0))
                for e in seqs]

    def centre(ys):
        means = [_group_sum(y, ones_bd) * (1.0 / HEAD_DIM) for y in ys]
        return [y - m for y, m in zip(ys, means)]

    def variance(ycs):
        return [_group_sum(yc * yc, ones_bd) * (1.0 / HEAD_DIM) for yc in ycs]

    def emit(c, ycs, vrs):
        for e in seqs:
            y = ycs[e] * lax.rsqrt(vrs[e] + GN_EPS) * prm(_P_GN_G) + prm(_P_GN_B) + bonus_s[e, rows_of(c), :].astype(F32)
            z = c_ref[e, rows_of(c), 3 * gw:4 * gw].astype(F32)
            o_ref[e, rows_of(c), :] = (y * _silu(z)).astype(o_ref.dtype)

    def step(c, states):
        prs = project(c, states)
        ys = readout(c - 1)
        older = [yc_s[e, c & 1] for e in seqs]
        vrs = variance(older)
        dss = correct(c, prs)
        ycs = centre(ys)
        new_states = advance(c, states, dss)
        for e in seqs:
            yc_s[e, (c - 1) & 1] = ycs[e]
        emit(c - 2, older, vrs)
        return new_states

    zero_states = tuple(jnp.zeros((gw, gw), F32) for _ in seqs)
    states = advance(0, zero_states, correct(0, project(0, zero_states)))
    first = centre(readout(0))
    states = advance(1, states, correct(1, project(1, states)))
    for e in seqs:
        yc_s[e, 0] = first[e]
    states = lax.fori_loop(2, nc, step, states, unroll=RWKV_STEP_UNROLL)
    last = centre(readout(nc - 1))
    older = [yc_s[e, (nc - 2) & 1] for e in seqs]
    emit(nc - 2, older, variance(older))
    emit(nc - 1, last, variance(last))


def _rwkv(cx, prm, w2a2, layer):
    b, s, _ = cx.shape
    gw = GROUP_WIDTH
    T = RWKV_CHUNK
    nc = s // T
    n_seq = RWKV_PAIR if b % RWKV_PAIR == 0 else 1
    assert nc % RWKV_GROUP == 0
    per_chunk = lambda r, w, dt: pltpu.VMEM((n_seq, nc, r, w), dt)
    return _layered_call(
        _rwkv_kernel,
        grid=(b // n_seq,),
        in_specs=[_row_spec((n_seq, s, SEG_C)), _layer_spec((_P_ROWS, gw)), _layer_spec((2 * LORA, 2 * gw))],
        out_specs=_row_spec((n_seq, s, gw)),
        out_shape=jax.ShapeDtypeStruct((b, s, gw), BF16),
        scratch_shapes=[
            pltpu.VMEM((n_seq, s, gw), BF16), pltpu.VMEM((n_seq, s, gw), BF16),
            per_chunk(2 * T, gw, BF16), per_chunk(2 * T, gw, BF16),
            per_chunk(T, gw, BF16),
            per_chunk(T, 2 * N_HEADS * T, BF16), per_chunk(SUBLANES, gw, F32),
            per_chunk(T, gw, BF16), per_chunk(T, gw, BF16),
            pltpu.VMEM((n_seq, 2, T, gw), F32),
        ],
        name="rwkv",
    )(layer, cx, prm, w2a2)


def _fox_kernel(d_ref, g_ref, qkg_ref, o_ref, q_s, k_s, vt_s, ft_s, l_s):
    s_len = d_ref.shape[0]
    gw = GROUP_WIDTH
    tq = min(FOX_QBLOCK, s_len)
    ones_bd = _block_ones(gw)

    def head_norm(u, g):
        ms = _group_sum(u * u, ones_bd) * (1.0 / HEAD_DIM)
        return u * lax.rsqrt(ms + NORM_EPS) * g

    q_s[...] = (head_norm(d_ref[:, 0:gw].astype(F32), qkg_ref[0:1, :]) * (LOG2_E * HEAD_DIM ** -0.5)).astype(BF16)
    k_s[...] = head_norm(d_ref[:, gw:2 * gw].astype(F32), qkg_ref[1:2, :]).astype(BF16)
    v_t = jnp.transpose(d_ref[:, 2 * gw:3 * gw].astype(F32)).astype(BF16)
    ones_rows = jnp.where(_row((BF16_ROWS, s_len)) == 0, 1.0, 0.0).astype(BF16)
    for h in range(N_HEADS):
        vt_s[h, 0:HEAD_DIM, :] = v_t[h * HEAD_DIM:(h + 1) * HEAD_DIM, :]
        vt_s[h, HEAD_DIM:, :] = ones_rows
    ft_s[...] = jnp.transpose(_cumsum_rows(_log_sigmoid(g_ref[...]), s_len) * LOG2_E)

    heads = range(N_HEADS)
    tril = _lane((tq, tq)) <= _row((tq, tq))
    for blk in range(s_len // tq):
        q0 = blk * tq
        q_st = _stack_heads(q_s[q0:q0 + tq, :])
        mx = [None] * N_HEADS
        for j in range(blk + 1):
            k0 = j * tq
            scores = _nt(q_st, k_s[k0:k0 + tq, :])
            for h in heads:
                lg = scores[h * tq:(h + 1) * tq, :] - ft_s[GATE_D + h:GATE_D + h + 1, k0:k0 + tq]
                if j == blk:
                    lg = jnp.where(tril, lg, -jnp.inf)
                l_s[h, :, k0:k0 + tq] = lg
                tile_max = jnp.max(lg, axis=-1, keepdims=True)
                mx[h] = tile_max if j == 0 else jnp.maximum(mx[h], tile_max)
        outs = []
        for h in heads:
            acc = None
            for j in range(blk + 1):
                k0 = j * tq
                p = jnp.exp2(l_s[h, :, k0:k0 + tq] - mx[h]).astype(BF16)
                part = _nt(vt_s[h, :, k0:k0 + tq], p)
                acc = part if acc is None else acc + part
            outs.append(acc[0:HEAD_DIM, :] * (1.0 / acc[HEAD_DIM:HEAD_DIM + 1, :]))
        out = jnp.transpose(jnp.concatenate(outs, axis=0))
        z = d_ref[q0:q0 + tq, 3 * gw:4 * gw].astype(F32)
        o_ref[q0:q0 + tq, :] = (out * _silu(z)).astype(o_ref.dtype)


def _fox(dx, gates, qkg, layer):
    b, s, _ = dx.shape
    gw = GROUP_WIDTH
    return _layered_call(
        _fox_kernel,
        grid=(b,),
        in_specs=[_row_spec((None, s, SEG_D)), _row_spec((None, s, SEG_G)), _layer_spec((2, gw))],
        out_specs=_row_spec((None, s, gw)),
        out_shape=jax.ShapeDtypeStruct((b, s, gw), BF16),
        scratch_shapes=[pltpu.VMEM((s, gw), BF16), pltpu.VMEM((s, gw), BF16),
                        pltpu.VMEM((N_HEADS, HEAD_DIM + BF16_ROWS, s), BF16), pltpu.VMEM((SEG_G, s), F32),
                        pltpu.VMEM((N_HEADS, min(FOX_QBLOCK, s), s), F32)],
        name="fox",
    )(layer, dx, gates, qkg)


def _mem_kv_kernel(mem_ref, g_ref, w_ref, o_ref):
    b, n_mem, d = mem_ref.shape
    mn = _rms_rows(mem_ref[...].reshape(b * n_mem, d), g_ref[...]).astype(BF16)
    o_ref[...] = _nn(mn, w_ref[...]).astype(o_ref.dtype).reshape(o_ref.shape)


def _mem_kv(mem, g, wkv):
    b, n_mem, d = mem.shape
    depth = wkv.shape[0]
    return pl.pallas_call(
        _mem_kv_kernel,
        grid=(depth,),
        in_specs=[
            pl.BlockSpec((b, n_mem, d), lambda l: (0, 0, 0)),
            pl.BlockSpec((1, d), lambda l: (0, 0)),
            pl.BlockSpec((None, d, 2 * d), lambda l: (l, 0, 0)),
        ],
        out_specs=pl.BlockSpec((None, b, n_mem, 2 * d), lambda l: (l, 0, 0, 0)),
        out_shape=jax.ShapeDtypeStruct((depth, b, n_mem, 2 * d), BF16),
        compiler_params=_params(),
        name="mem_kv",
    )(mem, g, wkv)


def _out_xattn_kernel(x_ref, ya_ref, yb_ref, yc_ref, yd_ref, wout_ref, g_ref, wq_ref, kv_ref, wo_ref,
                      o_ref, *, n_mem_heads):
    d = x_ref.shape[-1]
    dh = d // n_mem_heads
    tm = x_ref.shape[0]
    gain = lambda i: g_ref[i:i + 1, :]
    parts = [pl.ds(j * (tm // XATTN_PARTS), tm // XATTN_PARTS) for j in range(XATTN_PARTS)]
    ys = [jnp.concatenate([ya_ref[r, :], yb_ref[r, :], yc_ref[r, :], yd_ref[r, :]], axis=-1) for r in parts]
    mixed = [_nn(y, wout_ref[...]) for y in ys]
    xs = [x_ref[r, :] + _rms_rows(m, gain(_G_POST)) for r, m in zip(parts, mixed)]
    hms = [_rms_rows(x, gain(_G_XPRE)).astype(BF16) for x in xs]
    qs = [_nn(hm, wq_ref[...]).astype(BF16) for hm in hms]
    outs = [[] for _ in parts]
    for h in range(n_mem_heads):
        kh = kv_ref[:, h * dh:(h + 1) * dh]
        vh = kv_ref[:, d + h * dh:d + (h + 1) * dh]
        logits = [_nt(q[:, h * dh:(h + 1) * dh], kh) * (dh ** -0.5) for q in qs]
        ps = [jnp.exp(l - jnp.max(l, axis=-1, keepdims=True)) for l in logits]
        ps = [(p / jnp.sum(p, axis=-1, keepdims=True)).astype(BF16) for p in ps]
        for j, p in enumerate(ps):
            outs[j].append(_nn(p, vh).astype(BF16))
    attn = [_nn(jnp.concatenate(o, axis=-1), wo_ref[...]) for o in outs]
    for r, x, o in zip(parts, xs, attn):
        o_ref[r, :] = x + _rms_rows(o, gain(_G_XPOST))


def _out_xattn(x2, ys, wout, gains, wq, kv, wo, seq_len, layer, n_mem_heads=4):
    m, d = x2.shape
    gw = GROUP_WIDTH
    tm = min(XATTN_ROW_TILE, seq_len)
    per_seq = seq_len // tm
    n_mem = kv.shape[2]
    return _layered_call(
        functools.partial(_out_xattn_kernel, n_mem_heads=n_mem_heads),
        grid=(m // tm,),
        in_specs=[_row_spec((tm, d))] + [_row_spec((tm, gw))] * 4 + [
            _layer_spec((4 * gw, d)),
            _layer_spec((_G_ROWS, d)),
            _layer_spec((d, d)),
            pl.BlockSpec((None, None, n_mem, 2 * d), lambda i, layer: (layer[0], i // per_seq, 0, 0)),
            _layer_spec((d, d)),
        ],
        out_specs=_row_spec((tm, d)),
        out_shape=jax.ShapeDtypeStruct((m, d), F32),
        name="out_xattn",
    )(layer, x2, *ys, wout, gains, wq, kv, wo)


_IN_SIZES = (("a_q", GROUP_WIDTH), ("a_k", GROUP_WIDTH), ("a_v", GROUP_WIDTH), ("a_i", N_HEADS), ("a_f", N_HEADS),
             ("a_z", GROUP_WIDTH), ("b_x", GROUP_WIDTH), ("b_z", GROUP_WIDTH),
             ("c_r", GROUP_WIDTH), ("c_k", GROUP_WIDTH), ("c_v", GROUP_WIDTH), ("c_w", LORA), ("c_a", LORA),
             ("c_z", GROUP_WIDTH), ("d_q", GROUP_WIDTH), ("d_k", GROUP_WIDTH), ("d_v", GROUP_WIDTH),
             ("d_f", N_HEADS), ("d_z", GROUP_WIDTH))
_IN_ORDER = ("a_q", "a_k", "a_v", "a_z", "b_x", "b_z", "c_r", "c_k", "c_v", "c_z", "c_w", "c_a",
             "d_q", "d_k", "d_v", "d_z", "a_i", "a_f", "d_f")
N_IN = sum(size for _, size in _IN_SIZES)


def _relayout_in(w, axis):
    pieces, off = {}, 0
    for name, size in _IN_SIZES:
        pieces[name] = lax.slice_in_dim(w, off, off + size, axis=axis)
        off += size
    pad_shape = list(w.shape)
    pad_shape[axis] = SEG_G - 3 * N_HEADS
    return jnp.concatenate([pieces[n] for n in _IN_ORDER] + [jnp.zeros(pad_shape, w.dtype)], axis=axis)


def _relayout_w_kernel(w_ref, o_ref):
    src = {}
    off = 0
    for name, size in _IN_SIZES:
        src[name] = (off, size)
        off += size
    dst = 0
    gate_rows = []
    for name in _IN_ORDER:
        start, size = src[name]
        if size < BF16_ROWS:
            gate_rows.append(w_ref[start:start + size, :])
        else:
            o_ref[dst:dst + size, :] = w_ref[start:start + size, :].astype(o_ref.dtype)
            dst += size
    used = sum(g.shape[0] for g in gate_rows)
    gate_rows.append(jnp.zeros((o_ref.shape[0] - dst - used, o_ref.shape[1]), F32))
    o_ref[dst:, :] = jnp.concatenate(gate_rows, axis=0).astype(o_ref.dtype)


def _relayout_w(w_t):
    depth, n_in, d = w_t.shape
    assert n_in == N_IN
    tc = RELAYOUT_COLS
    return pl.pallas_call(
        _relayout_w_kernel,
        grid=(depth, d // tc),
        in_specs=[pl.BlockSpec((None, n_in, tc), lambda l, i: (l, 0, i))],
        out_specs=pl.BlockSpec((None, N_PROJ, tc), lambda l, i: (l, 0, i)),
        out_shape=jax.ShapeDtypeStruct((depth, N_PROJ, d), BF16),
        compiler_params=_params(),
        name="relayout_w",
    )(w_t)


def _block_diag(blocks):
    n, k, _ = blocks.shape
    eye = jnp.eye(n, dtype=blocks.dtype)
    return jnp.einsum("gcd,gh->gchd", blocks, eye).reshape(n * k, n * k)


def kernel(x, mem, pre_norm_g, post_norm_g, w_in, b_in, conv_a, norm_a_g, pool_w, pool_scale, shift_mu_c,
           decay_w0, decay_w2, iclr_a0, iclr_a2, key_k, key_a, bonus_u, gn_c_g, gn_c_b, qk_norm_d, w_out,
           mem_norm_g, xattn_pre_g, xattn_post_g, xattn_wq, xattn_wkv, xattn_wo):
    bsz, seq, d = x.shape
    depth = w_in.shape[0]
    gw = GROUP_WIDTH

    w_in_r = _relayout_w(jnp.swapaxes(w_in, 1, 2))
    b_in_r = _relayout_in(b_in, 1).reshape(depth, 1, N_PROJ)
    gains = jnp.stack([pre_norm_g, post_norm_g, xattn_pre_g, xattn_post_g]
                      + [jnp.zeros_like(pre_norm_g)] * (_G_ROWS - 4), axis=1)
    mu = shift_mu_c
    zeros_row = jnp.zeros((depth, gw), F32)
    mu_wa = jnp.concatenate([mu[:, 3 * gw:], jnp.zeros((depth, gw - 2 * LORA), F32)], axis=-1)
    rows = [mu[:, 0:gw], mu[:, gw:2 * gw], mu[:, 2 * gw:3 * gw], decay_w0, iclr_a0, key_k, key_a, bonus_u,
            gn_c_g, gn_c_b, mu_wa] + [zeros_row] * (_P_ROWS - 11)
    rwkv_prm = jnp.stack(rows, axis=1)
    zl = jnp.zeros((depth, LORA, gw), F32)
    w2a2 = jnp.concatenate([jnp.concatenate([decay_w2, zl], axis=-1),
                            jnp.concatenate([zl, iclr_a2], axis=-1)], axis=1).astype(BF16)
    pool_bd = jax.vmap(_block_diag)(pool_w).astype(BF16)
    qkg = jnp.tile(qk_norm_d, (1, 1, N_HEADS))
    norm_a = norm_a_g.reshape(depth, 1, gw)
    pool_sc = pool_scale.reshape(depth, 1, gw)
    w_out_b = w_out.astype(BF16)
    wq_b = xattn_wq.astype(BF16)
    wo_b = xattn_wo.astype(BF16)

    kv_all = _mem_kv(mem, mem_norm_g.reshape(1, d), xattn_wkv.astype(BF16))

    x2 = x.reshape(bsz * seq, d)
    sh = lambda u: u.reshape(bsz, seq, u.shape[-1])
    for l in range(depth):
        layer = jnp.full((1,), l, jnp.int32)
        pa, pb, pc, pd, pg = _in_proj(x2, gains, w_in_r, b_in_r, layer)
        gates = sh(pg)
        y_a = _mlstm(sh(pa), gates, conv_a, norm_a, layer)
        y_b = _pool(sh(pb), pool_bd, pool_sc, layer)
        y_c = _rwkv(sh(pc), rwkv_prm, w2a2, layer)
        y_d = _fox(sh(pd), gates, qkg, layer)
        ys = [u.reshape(bsz * seq, gw) for u in (y_a, y_b, y_c, y_d)]
        x2 = _out_xattn(x2, ys, w_out_b, gains, wq_b, kv_all, wo_b, seq, layer)
    return x2.reshape(bsz, seq, d)
```

```python
import functools

import jax
import jax.numpy as jnp
from jax import lax
from jax.experimental import pallas as pl
from jax.experimental.pallas import tpu as pltpu

F32 = jnp.float32
BF16 = jnp.bfloat16

HEAD_DIM = 64
N_HEADS = 4
GROUP_WIDTH = HEAD_DIM * N_HEADS
LANES = 128
SUBLANES = 8
BF16_ROWS = 16
NORM_EPS = 1e-6
GN_EPS = 64e-5
LOG2_E = 1.4426950408889634
POOL_WINDOWS = (2, 4, 8, 16)
CONV_WIDTH_A = 4
LORA = 64
MLSTM_CHUNK = 128
MLSTM_GROUP = 4
RWKV_CHUNK = 64
RWKV_SUB = 16
RWKV_GROUP = 8
RWKV_STEP_UNROLL = 6
RWKV_PAIR = 2
DECAY_SCALE = 0.6065306597126334
FOX_QBLOCK = 256
ROW_TILE = 1024
XATTN_ROW_TILE = 1024
XATTN_PARTS = 4
RELAYOUT_COLS = 256
VMEM_LIMIT = 56 * 1024 * 1024

SEG_A = 4 * GROUP_WIDTH
SEG_B = 2 * GROUP_WIDTH
SEG_C = 4 * GROUP_WIDTH + 2 * LORA
SEG_D = 4 * GROUP_WIDTH
SEG_G = LANES
N_PROJ = SEG_A + SEG_B + SEG_C + SEG_D + SEG_G
GATE_I, GATE_F, GATE_D = 0, N_HEADS, 2 * N_HEADS

(_G_PRE, _G_POST, _G_XPRE, _G_XPOST) = range(4)
_G_ROWS = 8

(_P_MU_R, _P_MU_K, _P_MU_V, _P_W0, _P_A0, _P_KEY_K, _P_KEY_A, _P_BONUS, _P_GN_G, _P_GN_B, _P_MU_WA) = range(11)
_P_ROWS = 16


def _lane(shape):
    return lax.broadcasted_iota(jnp.int32, shape, len(shape) - 1)


def _row(shape):
    return lax.broadcasted_iota(jnp.int32, shape, len(shape) - 2)


def _nt(a, b):
    return lax.dot_general(a, b, (((1,), (1,)), ((), ())), preferred_element_type=F32)


def _tn(a, b):
    return lax.dot_general(a, b, (((0,), (0,)), ((), ())), preferred_element_type=F32)


def _nn(a, b):
    return jnp.dot(a, b, preferred_element_type=F32)


def _sigmoid(x):
    return 1.0 / (1.0 + jnp.exp(-x))


def _silu(x):
    return x * _sigmoid(x)


def _softplus(x):
    return jnp.maximum(x, 0.0) + jnp.log(1.0 + jnp.exp(-jnp.abs(x)))


def _log_sigmoid(x):
    return -_softplus(-x)


def _cumsum_rows(x, seg):
    pos = _row(x.shape) & (seg - 1)
    s = 1
    while s < seg:
        x = x + jnp.where(pos >= s, pltpu.roll(x, shift=s, axis=0), 0.0)
        s *= 2
    return x


def _cummax_rows(x, seg):
    pos = _row(x.shape) & (seg - 1)
    s = 1
    while s < seg:
        x = jnp.maximum(x, jnp.where(pos >= s, pltpu.roll(x, shift=s, axis=0), -jnp.inf))
        s *= 2
    return x


def _stack_heads(x):
    hid = _lane(x.shape) >> 6
    zero = jnp.zeros_like(x)
    return jnp.concatenate([jnp.where(hid == h, x, zero) for h in range(N_HEADS)], axis=0)


def _per_head(cols, shape):
    hid = _lane(shape) >> 6
    out = jnp.broadcast_to(cols[N_HEADS - 1], shape)
    for h in range(N_HEADS - 2, -1, -1):
        out = jnp.where(hid == h, cols[h], out)
    return out


def _head_cols(x, first):
    return [x[:, first + h:first + h + 1] for h in range(N_HEADS)]


def _block_ones(n):
    return jnp.where((_row((n, n)) >> 6) == (_lane((n, n)) >> 6), 1.0, 0.0).astype(BF16)


def _group_sum(x, ones_bd):
    hi = x.astype(BF16)
    lo = (x - hi.astype(F32)).astype(BF16)
    return _nn(hi, ones_bd) + _nn(lo, ones_bd)


def _rms_rows(x, g):
    return x * lax.rsqrt(jnp.mean(x * x, axis=-1, keepdims=True) + NORM_EPS) * g


def _params(**kw):
    return pltpu.CompilerParams(vmem_limit_bytes=VMEM_LIMIT, **kw)


def _layer_spec(shape):
    zeros = (0,) * len(shape)
    return pl.BlockSpec((None,) + tuple(shape), lambda i, layer: (layer[0],) + zeros)


def _row_spec(shape):
    zeros = (0,) * (len(shape) - 1)
    return pl.BlockSpec(tuple(shape), lambda i, layer: (i,) + zeros)


def _layered_call(body, grid, in_specs, out_specs, out_shape, scratch_shapes=(), name=None):
    def with_layer(layer_ref, *refs):
        del layer_ref
        body(*refs)

    return pl.pallas_call(
        with_layer,
        grid_spec=pltpu.PrefetchScalarGridSpec(num_scalar_prefetch=1, grid=grid, in_specs=in_specs,
                                               out_specs=out_specs, scratch_shapes=scratch_shapes),
        out_shape=out_shape, compiler_params=_params(), name=name)


def _in_proj_kernel(x_ref, g_ref, w_ref, b_ref, oa_ref, ob_ref, oc_ref, od_ref, og_ref):
    h = _rms_rows(x_ref[...], g_ref[_G_PRE:_G_PRE + 1, :]).astype(BF16)
    off = 0
    for o_ref, width in ((oa_ref, SEG_A), (ob_ref, SEG_B), (oc_ref, SEG_C), (od_ref, SEG_D), (og_ref, SEG_G)):
        acc = _nt(h, w_ref[off:off + width, :]) + b_ref[:, off:off + width]
        o_ref[...] = acc.astype(o_ref.dtype)
        off += width


def _in_proj(x2, gains, w, b, layer):
    m, d = x2.shape
    tm = min(ROW_TILE, m)
    widths = (SEG_A, SEG_B, SEG_C, SEG_D, SEG_G)
    dtypes = (BF16, BF16, BF16, BF16, F32)
    return _layered_call(
        _in_proj_kernel,
        grid=(m // tm,),
        in_specs=[_row_spec((tm, d)), _layer_spec((_G_ROWS, d)), _layer_spec((N_PROJ, d)), _layer_spec((1, N_PROJ))],
        out_specs=[_row_spec((tm, wd)) for wd in widths],
        out_shape=[jax.ShapeDtypeStruct((m, wd), dt) for wd, dt in zip(widths, dtypes)],
        name="in_proj",
    )(layer, x2, gains, w, b)


def _mlstm_kernel(a_ref, g_ref, conv_ref, ng_ref, o_ref,
                  q_s, k_s, bc_s, gt_s, bt_s, cl_s, nl_s, ml_s):
    s_len = a_ref.shape[0]
    L = MLSTM_CHUNK
    nc = s_len // L
    grp = MLSTM_GROUP
    gw = GROUP_WIDTH

    rows_of = lambda c: pl.ds(pl.multiple_of(c * L, L), L)
    taps = CONV_WIDTH_A - 1
    shift_row = _row((taps * L, 2 * L))
    shift_mats = jnp.where(_lane((taps * L, 2 * L)) == L + (shift_row & (L - 1)) - 1 - shift_row // L, 1.0, 0.0).astype(BF16)

    def conv_and_gates(c, carry):
        r0 = pl.multiple_of(c * L, L)
        cur = a_ref[rows_of(c), :2 * gw]
        before = a_ref[pl.ds(pl.multiple_of(jnp.maximum(r0 - L, 0), L), L), :2 * gw]
        before = before * jnp.where(r0 > 0, 1.0, 0.0).astype(BF16)
        shifted = _nn(shift_mats, jnp.concatenate([before, cur], axis=0))
        acc = cur.astype(F32) * conv_ref[CONV_WIDTH_A - 1:CONV_WIDTH_A, :]
        for j in range(1, CONV_WIDTH_A):
            acc = acc + shifted[(j - 1) * L:j * L, :] * conv_ref[CONV_WIDTH_A - 1 - j:CONV_WIDTH_A - j, :]
        acc = _silu(acc)
        q_s[rows_of(c), :] = acc[:, :gw].astype(BF16)
        k_s[rows_of(c), :] = (acc[:, gw:] * (HEAD_DIM ** -0.5)).astype(BF16)
        gates = g_ref[rows_of(c), :]
        bcum = _cumsum_rows(_log_sigmoid(gates), L)
        bc_s[rows_of(c), :] = bcum
        gt_s[c] = jnp.transpose(gates)
        bt_s[c] = jnp.transpose(bcum)
        return carry

    ones_bd = _block_ones(gw)
    ones_rows = jnp.where((_row((N_HEADS * L, gw)) // L) == (_lane((N_HEADS * L, gw)) >> 6), 1.0, 0.0).astype(BF16)
    tril = _lane((L, L)) <= _row((L, L))
    bd_mask = (_row((gw, gw)) >> 6) == (_lane((gw, gw)) >> 6)
    rows8 = lambda u: jnp.broadcast_to(u, (SUBLANES, u.shape[-1]))

    def local_state(i, carry):
        cs = [i * grp + j for j in range(grp)]
        for c in cs:
            conv_and_gates(c, carry)
        ks = [k_s[rows_of(c), :] for c in cs]
        e_fulls, ves = [], []
        for c in cs:
            v = a_ref[rows_of(c), 2 * gw:3 * gw]
            ig = pltpu.roll(g_ref[rows_of(c), :], shift=GATE_F - GATE_I, axis=1)
            w_loc = bc_s[pl.ds(pl.multiple_of(c * L, L) + L - 1, 1), :] - bc_s[rows_of(c), :] + ig
            m_loc = jnp.max(w_loc, axis=0, keepdims=True)
            ml_s[c] = rows8(m_loc)
            e_full = _per_head(_head_cols(jnp.exp(w_loc - m_loc), GATE_F), (L, gw))
            e_fulls.append(e_full)
            ves.append((v.astype(F32) * e_full).astype(BF16))
        c_locs = [_tn(ve, k) for ve, k in zip(ves, ks)]
        for c, c_loc, e_full, k in zip(cs, c_locs, e_fulls, ks):
            cl_s[c] = jnp.where(bd_mask, c_loc, 0.0)
            nl_s[c] = rows8(jnp.sum(e_full * k.astype(F32), axis=0, keepdims=True))
        return carry

    lax.fori_loop(0, nc // grp, local_state, 0)

    def scan(c, carry):
        c_prev, n_prev, m_prev = carry
        g_tot = bc_s[pl.ds(pl.multiple_of(c * L, L) + L - 1, 1), :]
        m_loc = ml_s[c][0:1, :]
        m_new = jnp.maximum(g_tot + m_prev, m_loc)
        a_row = jnp.exp(g_tot + m_prev - m_new)
        b_row = jnp.exp(m_loc - m_new)
        a_h, b_h = _head_cols(a_row, GATE_F), _head_cols(b_row, GATE_F)
        c_loc = cl_s[c]
        head_rows = lambda u, h: u[h * HEAD_DIM:(h + 1) * HEAD_DIM, :]
        c_new = jnp.concatenate([a_h[h] * head_rows(c_prev, h) + b_h[h] * head_rows(c_loc, h) for h in range(N_HEADS)], axis=0)
        n_new = _per_head(a_h, (1, gw)) * n_prev + _per_head(b_h, (1, gw)) * nl_s[c][0:1, :]
        return c_new, n_new, m_new

    def outputs(i, carry):
        cs = [i * grp + j for j in range(grp)]
        states = [carry]
        for c in cs:
            states.append(scan(c, states[-1]))
        hid = _lane((L, gw)) >> 6
        qs = [q_s[rows_of(c), :] for c in cs]
        ks = [k_s[rows_of(c), :] for c in cs]
        vs = [a_ref[rows_of(c), 2 * gw:3 * gw] for c in cs]
        logits = [[_nt(jnp.where(hid == h, q, jnp.zeros_like(q)), k) for h in range(N_HEADS)] for q, k in zip(qs, ks)]
        num_inter = [_nt(q, st[0].astype(BF16)) for q, st in zip(qs, states)]
        den_inter = [_group_sum(q.astype(F32) * st[1], ones_bd) for q, st in zip(qs, states)]
        shape = (L, gw)
        s_cat, iws, floors = [], [], []
        for j, c in enumerate(cs):
            b_col = bc_s[rows_of(c), :]
            g_row = gt_s[c]
            b_row = bt_s[c]
            ig = pltpu.roll(g_ref[rows_of(c), :], shift=GATE_F - GATE_I, axis=1)
            m_inter = b_col + states[j][2]
            m_t = jnp.maximum(m_inter, b_col + _cummax_rows(ig - b_col, L))
            iws.append(_per_head(_head_cols(jnp.exp(m_inter - m_t), GATE_F), shape))
            floors.append(_per_head(_head_cols(jnp.exp(-m_t), GATE_F), shape))
            col_arg = b_col - m_t
            s_list = []
            for h in range(N_HEADS):
                row_arg = g_row[GATE_I + h:GATE_I + h + 1, :] - b_row[GATE_F + h:GATE_F + h + 1, :]
                arg = jnp.where(tril, col_arg[:, GATE_F + h:GATE_F + h + 1] + row_arg, -jnp.inf)
                s_list.append((logits[j][h] * jnp.exp(arg)).astype(BF16))
            s_cat.append(jnp.concatenate(s_list, axis=1))
        num_intra = [_nn(s, _stack_heads(v)) for s, v in zip(s_cat, vs)]
        den_intra = [_nn(s, ones_rows) for s in s_cat]
        hhs = []
        for j in range(grp):
            num = num_intra[j] + iws[j] * num_inter[j]
            den = den_intra[j] + iws[j] * den_inter[j]
            hhs.append(num / jnp.maximum(jnp.abs(den), floors[j]))
        mss = [_group_sum(hh * hh, ones_bd) for hh in hhs]
        for c, hh, ms in zip(cs, hhs, mss):
            z = a_ref[rows_of(c), 3 * gw:4 * gw].astype(F32)
            y = hh * lax.rsqrt(ms * (1.0 / HEAD_DIM) + NORM_EPS) * ng_ref[...] * _silu(z)
            o_ref[rows_of(c), :] = y.astype(o_ref.dtype)
        return states[-1]

    lax.fori_loop(0, nc // grp, outputs,
                  (jnp.zeros((gw, gw), F32), jnp.zeros((1, gw), F32), jnp.zeros((1, LANES), F32)))


def _mlstm(a, gates, conv, ng, layer):
    b, s, _ = a.shape
    gw = GROUP_WIDTH
    L = MLSTM_CHUNK
    nc = s // L
    assert nc % MLSTM_GROUP == 0
    return _layered_call(
        _mlstm_kernel,
        grid=(b,),
        in_specs=[_row_spec((None, s, SEG_A)), _row_spec((None, s, SEG_G)),
                  _layer_spec((CONV_WIDTH_A, 2 * gw)), _layer_spec((1, gw))],
        out_specs=_row_spec((None, s, gw)),
        out_shape=jax.ShapeDtypeStruct((b, s, gw), BF16),
        scratch_shapes=[
            pltpu.VMEM((s, gw), BF16), pltpu.VMEM((s, gw), BF16),
            pltpu.VMEM((s, SEG_G), F32),
            pltpu.VMEM((nc, SEG_G, L), F32), pltpu.VMEM((nc, SEG_G, L), F32),
            pltpu.VMEM((nc, gw, gw), F32), pltpu.VMEM((nc, SUBLANES, gw), F32), pltpu.VMEM((nc, SUBLANES, LANES), F32),
        ],
        name="mlstm",
    )(layer, a, gates, conv, ng)


def _pool_kernel(b_ref, w_ref, sc_ref, o_ref):
    gw = GROUP_WIDTH
    u = b_ref[:, :gw].astype(F32)
    z = b_ref[:, gw:].astype(F32)
    head = POOL_WINDOWS[-1]
    grp = _lane((u.shape[0] + head, gw)) >> 6
    lane_grp = _lane((1, gw)) >> 6
    run, width = jnp.concatenate([jnp.zeros((head, gw), F32), u], axis=0), 1
    window = jnp.full((1, gw), float(POOL_WINDOWS[-1]), F32)
    sums = []
    for win in POOL_WINDOWS:
        while width < win:
            run = run + pltpu.roll(run, shift=width, axis=0)
            width *= 2
        sums.append(run)
    window_sum = sums[-1]
    for gi in range(len(POOL_WINDOWS) - 2, -1, -1):
        window_sum = jnp.where(grp == gi, sums[gi], window_sum)
        window = jnp.where(lane_grp == gi, float(POOL_WINDOWS[gi]), window)
    window_sum = window_sum[head:, :]
    pos = _row((head, gw)).astype(F32)
    first = window_sum[:head, :] / jnp.minimum(pos + 1.0, window) - u[:head, :]
    rest = window_sum[head:, :] * (1.0 / window) - u[head:, :]
    pooled = jnp.concatenate([first, rest], axis=0)
    mixed = _nn(pooled.astype(BF16), w_ref[...])
    o_ref[...] = (mixed * sc_ref[...] * _silu(z)).astype(o_ref.dtype)


def _pool(bx, w_bd, scale, layer):
    b, s, _ = bx.shape
    gw = GROUP_WIDTH
    return _layered_call(
        _pool_kernel,
        grid=(b,),
        in_specs=[_row_spec((None, s, SEG_B)), _layer_spec((gw, gw)), _layer_spec((1, gw))],
        out_specs=_row_spec((None, s, gw)),
        out_shape=jax.ShapeDtypeStruct((b, s, gw), BF16),
        name="pool",
    )(layer, bx, w_bd, scale)


def _cat_mm(xs, ys):
    return [_nn(x.astype(BF16), _stack_heads(y.astype(BF16))) for x, y in zip(xs, ys)]


def _unit_lower_inverse(mats):
    t = RWKV_CHUNK
    shift = RWKV_SUB.bit_length() - 1
    shape = mats[0].shape
    rb = _row(shape) >> shift
    col = _lane(shape) & (t - 1)
    cb = col >> shift
    eye = jnp.where(col == _row(shape), 1.0, 0.0)
    add = lambda xs, ys: [x + y for x, y in zip(xs, ys)]
    d = [jnp.where(rb == cb, a, 0.0) for a in mats]
    e = [jnp.where(rb == cb, 0.0, a) for a in mats]
    d2 = _cat_mm(d, d)
    d4 = _cat_mm(d2, d2)
    x = [eye + di for di in d]
    x = add(x, _cat_mm(x, d2))
    d8 = _cat_mm(d4, d4)
    x = add(x, _cat_mm(x, d4))
    dinv = add(x, _cat_mm(x, d8))
    f = _cat_mm(dinv, e)
    f2 = _cat_mm(f, f)
    w = [eye + fi for fi in f]
    w = add(w, _cat_mm(w, f2))
    return _cat_mm(w, dinv)


def _rwkv_kernel(c_ref, p_ref, w2a2_ref, o_ref,
                 v_s, bonus_s, pr_s, qk_s, ul_s, ar_s, ge_s, ub_s, rs_s, yc_s):
    n_seq, s_len = c_ref.shape[0], c_ref.shape[1]
    gw = GROUP_WIDTH
    T = RWKV_CHUNK
    grp = RWKV_GROUP
    rows = T * grp
    nc = s_len // T
    prm = lambda i: p_ref[i:i + 1, :]

    ones_bd = _block_ones(gw)
    col = _lane((T, N_HEADS * T)) & (T - 1)
    row = _row((T, N_HEADS * T))
    strict = col < row
    incl = col <= row
    bd_mask = (_row((gw, gw)) >> 6) == (_lane((gw, gw)) >> 6)
    first_row = _row((rows, gw)) == 0

    def prepare(e, i, carry):
        r0 = pl.multiple_of(i * rows, rows)
        tile = pl.ds(r0, rows)
        halo = c_ref[e, pl.ds(pl.multiple_of(jnp.maximum(r0 - BF16_ROWS, 0), BF16_ROWS), BF16_ROWS), :]
        prev = halo[BF16_ROWS - 1:BF16_ROWS, :].astype(F32) * jnp.where(r0 > 0, 1.0, 0.0)

        def shifted(lo, hi, mu):
            u = c_ref[e, tile, lo:hi].astype(F32)
            before = jnp.where(first_row[:, :hi - lo], prev[:, lo:hi], pltpu.roll(u, shift=1, axis=0))
            return u + mu * (before - u)

        wa = shifted(4 * gw, 4 * gw + 2 * LORA, p_ref[_P_MU_WA:_P_MU_WA + 1, :2 * LORA])
        wa = jnp.where(_lane(wa.shape) < LORA, jnp.tanh(wa), wa)
        lora = _nn(wa.astype(BF16), w2a2_ref[...])
        ld = -DECAY_SCALE * _sigmoid(prm(_P_W0) + lora[:, :gw])
        a = _sigmoid(prm(_P_A0) + lora[:, gw:])
        r = shifted(0, gw, prm(_P_MU_R))
        k = shifted(gw, 2 * gw, prm(_P_MU_K))
        v = shifted(2 * gw, 3 * gw, prm(_P_MU_V))
        kk = k * prm(_P_KEY_K)
        kk = kk * jnp.minimum(lax.rsqrt(_group_sum(kk * kk, ones_bd)), 1e12)
        k = k * (1.0 + (a - 1.0) * prm(_P_KEY_A))
        bonus_s[e, tile, :] = (_group_sum(r * k * prm(_P_BONUS), ones_bd) * v).astype(BF16)
        vb = v.astype(BF16)
        v_s[e, tile, :] = vb

        lc = _cumsum_rows(ld, T)
        g_inc = jnp.exp(lc)
        g_inv = jnp.exp(-lc)
        pt = (-kk * jnp.exp(lc - ld)).astype(BF16)
        rt = (r * g_inc).astype(BF16)
        qt = (kk * a * g_inv).astype(BF16)
        kt = (k * g_inv).astype(BF16)

        part = lambda u, j: u[j * T:(j + 1) * T, :]
        chunks = range(grp)
        prs = [jnp.concatenate([part(pt, j), part(rt, j)], axis=0) for j in chunks]
        gs = [_nt(prs[j], jnp.concatenate([_stack_heads(part(qt, j)), _stack_heads(part(kt, j))], axis=0))
              for j in chunks]
        a_pq = [jnp.where(strict, g[:T, :N_HEADS * T], 0.0) for g in gs]
        a_pk = [jnp.where(strict, g[:T, N_HEADS * T:], 0.0).astype(BF16) for g in gs]
        apv = [_nn(a_pk[j], _stack_heads(part(vb, j))) for j in chunks]
        tinv = [t.astype(BF16) for t in _unit_lower_inverse(a_pq)]
        p_fold = [_nn(tinv[j], _stack_heads(part(pt, j))) for j in chunks]
        u_loc = [_nn(tinv[j], _stack_heads(apv[j].astype(BF16))) for j in chunks]
        for j in chunks:
            c = i * grp + j
            pr_s[e, c] = jnp.concatenate([p_fold[j].astype(BF16), part(rt, j)], axis=0)
            qk_s[e, c] = jnp.concatenate([part(qt, j), part(kt, j)], axis=0)
            ul_s[e, c] = u_loc[j].astype(BF16)
            ar_s[e, c] = jnp.concatenate([jnp.where(incl, gs[j][T:, :N_HEADS * T], 0.0),
                                          jnp.where(incl, gs[j][T:, N_HEADS * T:], 0.0)], axis=1).astype(BF16)
            ge_s[e, c] = jnp.broadcast_to(part(g_inc, j)[T - 1:T, :], (SUBLANES, gw))
        return carry

    for e in range(n_seq):
        lax.fori_loop(0, nc // grp, functools.partial(prepare, e), 0)

    seqs = range(n_seq)

    rows_of = lambda c: pl.ds(pl.multiple_of(c * T, T), T)

    def project(c, states):
        return [_nt(pr_s[e, c], states[e].astype(BF16)) for e in seqs]

    def correct(c, prs):
        ubs = [(prs[e][:T] + ul_s[e, c].astype(F32)).astype(BF16) for e in seqs]
        dss = [_tn(jnp.concatenate([ubs[e], v_s[e, rows_of(c), :]], axis=0), qk_s[e, c]) for e in seqs]
        for e in seqs:
            ub_s[e, c] = ubs[e]
            rs_s[e, c] = prs[e][T:].astype(BF16)
        return dss

    def advance(c, states, dss):
        return tuple((states[e] + jnp.where(bd_mask, dss[e], 0.0)) * ge_s[e, c][0:1, :] for e in seqs)

    def readout(c):
        return [rs_s[e, c].astype(F32)
                + _nn(ar_s[e, c], jnp.concatenate([_stack_heads(ub_s[e, c]), _stack_heads(v_s[e, rows_of(c), :])], axis=0))
                for e in seqs]

    def centre(ys):
        means = [_group_sum(y, ones_bd) * (1.0 / HEAD_DIM) for y in ys]
        return [y - m for y, m in zip(ys, means)]

    def variance(ycs):
        return [_group_sum(yc * yc, ones_bd) * (1.0 / HEAD_DIM) for yc in ycs]

    def emit(c, ycs, vrs):
        for e in seqs:
            y = ycs[e] * lax.rsqrt(vrs[e] + GN_EPS) * prm(_P_GN_G) + prm(_P_GN_B) + bonus_s[e, rows_of(c), :].astype(F32)
            z = c_ref[e, rows_of(c), 3 * gw:4 * gw].astype(F32)
            o_ref[e, rows_of(c), :] = (y * _silu(z)).astype(o_ref.dtype)

    def step(c, states):
        prs = project(c, states)
        ys = readout(c - 1)
        older = [yc_s[e, c & 1] for e in seqs]
        vrs = variance(older)
        dss = correct(c, prs)
        ycs = centre(ys)
        new_states = advance(c, states, dss)
        for e in seqs:
            yc_s[e, (c - 1) & 1] = ycs[e]
        emit(c - 2, older, vrs)
        return new_states

    zero_states = tuple(jnp.zeros((gw, gw), F32) for _ in seqs)
    states = advance(0, zero_states, correct(0, project(0, zero_states)))
    first = centre(readout(0))
    states = advance(1, states, correct(1, project(1, states)))
    for e in seqs:
        yc_s[e, 0] = first[e]
    states = lax.fori_loop(2, nc, step, states, unroll=RWKV_STEP_UNROLL)
    last = centre(readout(nc - 1))
    older = [yc_s[e, (nc - 2) & 1] for e in seqs]
    emit(nc - 2, older, variance(older))
    emit(nc - 1, last, variance(last))


def _rwkv(cx, prm, w2a2, layer):
    b, s, _ = cx.shape
    gw = GROUP_WIDTH
    T = RWKV_CHUNK
    nc = s // T
    n_seq = RWKV_PAIR if b % RWKV_PAIR == 0 else 1
    assert nc % RWKV_GROUP == 0
    per_chunk = lambda r, w, dt: pltpu.VMEM((n_seq, nc, r, w), dt)
    return _layered_call(
        _rwkv_kernel,
        grid=(b // n_seq,),
        in_specs=[_row_spec((n_seq, s, SEG_C)), _layer_spec((_P_ROWS, gw)), _layer_spec((2 * LORA, 2 * gw))],
        out_specs=_row_spec((n_seq, s, gw)),
        out_shape=jax.ShapeDtypeStruct((b, s, gw), BF16),
        scratch_shapes=[
            pltpu.VMEM((n_seq, s, gw), BF16), pltpu.VMEM((n_seq, s, gw), BF16),
            per_chunk(2 * T, gw, BF16), per_chunk(2 * T, gw, BF16),
            per_chunk(T, gw, BF16),
            per_chunk(T, 2 * N_HEADS * T, BF16), per_chunk(SUBLANES, gw, F32),
            per_chunk(T, gw, BF16), per_chunk(T, gw, BF16),
            pltpu.VMEM((n_seq, 2, T, gw), F32),
        ],
        name="rwkv",
    )(layer, cx, prm, w2a2)


def _fox_kernel(d_ref, g_ref, qkg_ref, o_ref, q_s, k_s, vt_s, ft_s, l_s):
    s_len = d_ref.shape[0]
    gw = GROUP_WIDTH
    tq = min(FOX_QBLOCK, s_len)
    ones_bd = _block_ones(gw)

    def head_norm(u, g):
        ms = _group_sum(u * u, ones_bd) * (1.0 / HEAD_DIM)
        return u * lax.rsqrt(ms + NORM_EPS) * g

    q_s[...] = (head_norm(d_ref[:, 0:gw].astype(F32), qkg_ref[0:1, :]) * (LOG2_E * HEAD_DIM ** -0.5)).astype(BF16)
    k_s[...] = head_norm(d_ref[:, gw:2 * gw].astype(F32), qkg_ref[1:2, :]).astype(BF16)
    v_t = jnp.transpose(d_ref[:, 2 * gw:3 * gw].astype(F32)).astype(BF16)
    ones_rows = jnp.where(_row((BF16_ROWS, s_len)) == 0, 1.0, 0.0).astype(BF16)
    for h in range(N_HEADS):
        vt_s[h, 0:HEAD_DIM, :] = v_t[h * HEAD_DIM:(h + 1) * HEAD_DIM, :]
        vt_s[h, HEAD_DIM:, :] = ones_rows
    ft_s[...] = jnp.transpose(_cumsum_rows(_log_sigmoid(g_ref[...]), s_len) * LOG2_E)

    heads = range(N_HEADS)
    tril = _lane((tq, tq)) <= _row((tq, tq))
    for blk in range(s_len // tq):
        q0 = blk * tq
        q_st = _stack_heads(q_s[q0:q0 + tq, :])
        mx = [None] * N_HEADS
        for j in range(blk + 1):
            k0 = j * tq
            scores = _nt(q_st, k_s[k0:k0 + tq, :])
            for h in heads:
                lg = scores[h * tq:(h + 1) * tq, :] - ft_s[GATE_D + h:GATE_D + h + 1, k0:k0 + tq]
                if j == blk:
                    lg = jnp.where(tril, lg, -jnp.inf)
                l_s[h, :, k0:k0 + tq] = lg
                tile_max = jnp.max(lg, axis=-1, keepdims=True)
                mx[h] = tile_max if j == 0 else jnp.maximum(mx[h], tile_max)
        outs = []
        for h in heads:
            acc = None
            for j in range(blk + 1):
                k0 = j * tq
                p = jnp.exp2(l_s[h, :, k0:k0 + tq] - mx[h]).astype(BF16)
                part = _nt(vt_s[h, :, k0:k0 + tq], p)
                acc = part if acc is None else acc + part
            outs.append(acc[0:HEAD_DIM, :] * (1.0 / acc[HEAD_DIM:HEAD_DIM + 1, :]))
        out = jnp.transpose(jnp.concatenate(outs, axis=0))
        z = d_ref[q0:q0 + tq, 3 * gw:4 * gw].astype(F32)
        o_ref[q0:q0 + tq, :] = (out * _silu(z)).astype(o_ref.dtype)


def _fox(dx, gates, qkg, layer):
    b, s, _ = dx.shape
    gw = GROUP_WIDTH
    return _layered_call(
        _fox_kernel,
        grid=(b,),
        in_specs=[_row_spec((None, s, SEG_D)), _row_spec((None, s, SEG_G)), _layer_spec((2, gw))],
        out_specs=_row_spec((None, s, gw)),
        out_shape=jax.ShapeDtypeStruct((b, s, gw), BF16),
        scratch_shapes=[pltpu.VMEM((s, gw), BF16), pltpu.VMEM((s, gw), BF16),
                        pltpu.VMEM((N_HEADS, HEAD_DIM + BF16_ROWS, s), BF16), pltpu.VMEM((SEG_G, s), F32),
                        pltpu.VMEM((N_HEADS, min(FOX_QBLOCK, s), s), F32)],
        name="fox",
    )(layer, dx, gates, qkg)


def _mem_kv_kernel(mem_ref, g_ref, w_ref, o_ref):
    b, n_mem, d = mem_ref.shape
    mn = _rms_rows(mem_ref[...].reshape(b * n_mem, d), g_ref[...]).astype(BF16)
    o_ref[...] = _nn(mn, w_ref[...]).astype(o_ref.dtype).reshape(o_ref.shape)


def _mem_kv(mem, g, wkv):
    b, n_mem, d = mem.shape
    depth = wkv.shape[0]
    return pl.pallas_call(
        _mem_kv_kernel,
        grid=(depth,),
        in_specs=[
            pl.BlockSpec((b, n_mem, d), lambda l: (0, 0, 0)),
            pl.BlockSpec((1, d), lambda l: (0, 0)),
            pl.BlockSpec((None, d, 2 * d), lambda l: (l, 0, 0)),
        ],
        out_specs=pl.BlockSpec((None, b, n_mem, 2 * d), lambda l: (l, 0, 0, 0)),
        out_shape=jax.ShapeDtypeStruct((depth, b, n_mem, 2 * d), BF16),
        compiler_params=_params(),
        name="mem_kv",
    )(mem, g, wkv)


def _out_xattn_kernel(x_ref, ya_ref, yb_ref, yc_ref, yd_ref, wout_ref, g_ref, wq_ref, kv_ref, wo_ref,
                      o_ref, *, n_mem_heads):
    d = x_ref.shape[-1]
    dh = d // n_mem_heads
    tm = x_ref.shape[0]
    gain = lambda i: g_ref[i:i + 1, :]
    parts = [pl.ds(j * (tm // XATTN_PARTS), tm // XATTN_PARTS) for j in range(XATTN_PARTS)]
    ys = [jnp.concatenate([ya_ref[r, :], yb_ref[r, :], yc_ref[r, :], yd_ref[r, :]], axis=-1) for r in parts]
    mixed = [_nn(y, wout_ref[...]) for y in ys]
    xs = [x_ref[r, :] + _rms_rows(m, gain(_G_POST)) for r, m in zip(parts, mixed)]
    hms = [_rms_rows(x, gain(_G_XPRE)).astype(BF16) for x in xs]
    qs = [_nn(hm, wq_ref[...]).astype(BF16) for hm in hms]
    outs = [[] for _ in parts]
    for h in range(n_mem_heads):
        kh = kv_ref[:, h * dh:(h + 1) * dh]
        vh = kv_ref[:, d + h * dh:d + (h + 1) * dh]
        logits = [_nt(q[:, h * dh:(h + 1) * dh], kh) * (dh ** -0.5) for q in qs]
        ps = [jnp.exp(l - jnp.max(l, axis=-1, keepdims=True)) for l in logits]
        ps = [(p / jnp.sum(p, axis=-1, keepdims=True)).astype(BF16) for p in ps]
        for j, p in enumerate(ps):
            outs[j].append(_nn(p, vh).astype(BF16))
    attn = [_nn(jnp.concatenate(o, axis=-1), wo_ref[...]) for o in outs]
    for r, x, o in zip(parts, xs, attn):
        o_ref[r, :] = x + _rms_rows(o, gain(_G_XPOST))


def _out_xattn(x2, ys, wout, gains, wq, kv, wo, seq_len, layer, n_mem_heads=4):
    m, d = x2.shape
    gw = GROUP_WIDTH
    tm = min(XATTN_ROW_TILE, seq_len)
    per_seq = seq_len // tm
    n_mem = kv.shape[2]
    return _layered_call(
        functools.partial(_out_xattn_kernel, n_mem_heads=n_mem_heads),
        grid=(m // tm,),
        in_specs=[_row_spec((tm, d))] + [_row_spec((tm, gw))] * 4 + [
            _layer_spec((4 * gw, d)),
            _layer_spec((_G_ROWS, d)),
            _layer_spec((d, d)),
            pl.BlockSpec((None, None, n_mem, 2 * d), lambda i, layer: (layer[0], i // per_seq, 0, 0)),
            _layer_spec((d, d)),
        ],
        out_specs=_row_spec((tm, d)),
        out_shape=jax.ShapeDtypeStruct((m, d), F32),
        name="out_xattn",
    )(layer, x2, *ys, wout, gains, wq, kv, wo)


_IN_SIZES = (("a_q", GROUP_WIDTH), ("a_k", GROUP_WIDTH), ("a_v", GROUP_WIDTH), ("a_i", N_HEADS), ("a_f", N_HEADS),
             ("a_z", GROUP_WIDTH), ("b_x", GROUP_WIDTH), ("b_z", GROUP_WIDTH),
             ("c_r", GROUP_WIDTH), ("c_k", GROUP_WIDTH), ("c_v", GROUP_WIDTH), ("c_w", LORA), ("c_a", LORA),
             ("c_z", GROUP_WIDTH), ("d_q", GROUP_WIDTH), ("d_k", GROUP_WIDTH), ("d_v", GROUP_WIDTH),
             ("d_f", N_HEADS), ("d_z", GROUP_WIDTH))
_IN_ORDER = ("a_q", "a_k", "a_v", "a_z", "b_x", "b_z", "c_r", "c_k", "c_v", "c_z", "c_w", "c_a",
             "d_q", "d_k", "d_v", "d_z", "a_i", "a_f", "d_f")
N_IN = sum(size for _, size in _IN_SIZES)


def _relayout_in(w, axis):
    pieces, off = {}, 0
    for name, size in _IN_SIZES:
        pieces[name] = lax.slice_in_dim(w, off, off + size, axis=axis)
        off += size
    pad_shape = list(w.shape)
    pad_shape[axis] = SEG_G - 3 * N_HEADS
    return jnp.concatenate([pieces[n] for n in _IN_ORDER] + [jnp.zeros(pad_shape, w.dtype)], axis=axis)


def _relayout_w_kernel(w_ref, o_ref):
    src = {}
    off = 0
    for name, size in _IN_SIZES:
        src[name] = (off, size)
        off += size
    dst = 0
    gate_rows = []
    for name in _IN_ORDER:
        start, size = src[name]
        if size < BF16_ROWS:
            gate_rows.append(w_ref[start:start + size, :])
        else:
            o_ref[dst:dst + size, :] = w_ref[start:start + size, :].astype(o_ref.dtype)
            dst += size
    used = sum(g.shape[0] for g in gate_rows)
    gate_rows.append(jnp.zeros((o_ref.shape[0] - dst - used, o_ref.shape[1]), F32))
    o_ref[dst:, :] = jnp.concatenate(gate_rows, axis=0).astype(o_ref.dtype)


def _relayout_w(w_t):
    depth, n_in, d = w_t.shape
    assert n_in == N_IN
    tc = RELAYOUT_COLS
    return pl.pallas_call(
        _relayout_w_kernel,
        grid=(depth, d // tc),
        in_specs=[pl.BlockSpec((None, n_in, tc), lambda l, i: (l, 0, i))],
        out_specs=pl.BlockSpec((None, N_PROJ, tc), lambda l, i: (l, 0, i)),
        out_shape=jax.ShapeDtypeStruct((depth, N_PROJ, d), BF16),
        compiler_params=_params(),
        name="relayout_w",
    )(w_t)


def _block_diag(blocks):
    n, k, _ = blocks.shape
    eye = jnp.eye(n, dtype=blocks.dtype)
    return jnp.einsum("gcd,gh->gchd", blocks, eye).reshape(n * k, n * k)


def kernel(x, mem, pre_norm_g, post_norm_g, w_in, b_in, conv_a, norm_a_g, pool_w, pool_scale, shift_mu_c,
           decay_w0, decay_w2, iclr_a0, iclr_a2, key_k, key_a, bonus_u, gn_c_g, gn_c_b, qk_norm_d, w_out,
           mem_norm_g, xattn_pre_g, xattn_post_g, xattn_wq, xattn_wkv, xattn_wo):
    bsz, seq, d = x.shape
    depth = w_in.shape[0]
    gw = GROUP_WIDTH

    w_in_r = _relayout_w(jnp.swapaxes(w_in, 1, 2))
    b_in_r = _relayout_in(b_in, 1).reshape(depth, 1, N_PROJ)
    gains = jnp.stack([pre_norm_g, post_norm_g, xattn_pre_g, xattn_post_g]
                      + [jnp.zeros_like(pre_norm_g)] * (_G_ROWS - 4), axis=1)
    mu = shift_mu_c
    zeros_row = jnp.zeros((depth, gw), F32)
    mu_wa = jnp.concatenate([mu[:, 3 * gw:], jnp.zeros((depth, gw - 2 * LORA), F32)], axis=-1)
    rows = [mu[:, 0:gw], mu[:, gw:2 * gw], mu[:, 2 * gw:3 * gw], decay_w0, iclr_a0, key_k, key_a, bonus_u,
            gn_c_g, gn_c_b, mu_wa] + [zeros_row] * (_P_ROWS - 11)
    rwkv_prm = jnp.stack(rows, axis=1)
    zl = jnp.zeros((depth, LORA, gw), F32)
    w2a2 = jnp.concatenate([jnp.concatenate([decay_w2, zl], axis=-1),
                            jnp.concatenate([zl, iclr_a2], axis=-1)], axis=1).astype(BF16)
    pool_bd = jax.vmap(_block_diag)(pool_w).astype(BF16)
    qkg = jnp.tile(qk_norm_d, (1, 1, N_HEADS))
    norm_a = norm_a_g.reshape(depth, 1, gw)
    pool_sc = pool_scale.reshape(depth, 1, gw)
    w_out_b = w_out.astype(BF16)
    wq_b = xattn_wq.astype(BF16)
    wo_b = xattn_wo.astype(BF16)

    kv_all = _mem_kv(mem, mem_norm_g.reshape(1, d), xattn_wkv.astype(BF16))

    x2 = x.reshape(bsz * seq, d)
    sh = lambda u: u.reshape(bsz, seq, u.shape[-1])
    for l in range(depth):
        layer = jnp.full((1,), l, jnp.int32)
        pa, pb, pc, pd, pg = _in_proj(x2, gains, w_in_r, b_in_r, layer)
        gates = sh(pg)
        y_a = _mlstm(sh(pa), gates, conv_a, norm_a, layer)
        y_b = _pool(sh(pb), pool_bd, pool_sc, layer)
        y_c = _rwkv(sh(pc), rwkv_prm, w2a2, layer)
        y_d = _fox(sh(pd), gates, qkg, layer)
        ys = [u.reshape(bsz * seq, gw) for u in (y_a, y_b, y_c, y_d)]
        x2 = _out_xattn(x2, ys, w_out_b, gains, wq_b, kv_all, wo_b, seq, layer)
    return x2.reshape(bsz, seq, d)
```

```python
import functools

import jax
import jax.numpy as jnp
from jax import lax
from jax.experimental import pallas as pl
from jax.experimental.pallas import tpu as pltpu

F32 = jnp.float32
BF16 = jnp.bfloat16

HEAD_DIM = 64
N_HEADS = 4
GROUP_WIDTH = HEAD_DIM * N_HEADS
LANES = 128
SUBLANES = 8
BF16_ROWS = 16
NORM_EPS = 1e-6
GN_EPS = 64e-5
LOG2_E = 1.4426950408889634
POOL_WINDOWS = (2, 4, 8, 16)
CONV_WIDTH_A = 4
LORA = 64
MLSTM_CHUNK = 128
MLSTM_GROUP = 4
RWKV_CHUNK = 64
RWKV_SUB = 16
RWKV_GROUP = 8
RWKV_STEP_UNROLL = 6
RWKV_PAIR = 2
DECAY_SCALE = 0.6065306597126334
FOX_QBLOCK = 256
ROW_TILE = 1024
XATTN_ROW_TILE = 1024
XATTN_PARTS = 4
RELAYOUT_COLS = 256
VMEM_LIMIT = 56 * 1024 * 1024

SEG_A = 4 * GROUP_WIDTH
SEG_B = 2 * GROUP_WIDTH
SEG_C = 4 * GROUP_WIDTH + 2 * LORA
SEG_D = 4 * GROUP_WIDTH
SEG_G = LANES
N_PROJ = SEG_A + SEG_B + SEG_C + SEG_D + SEG_G
GATE_I, GATE_F, GATE_D = 0, N_HEADS, 2 * N_HEADS

(_G_PRE, _G_POST, _G_XPRE, _G_XPOST) = range(4)
_G_ROWS = 8

(_P_MU_R, _P_MU_K, _P_MU_V, _P_W0, _P_A0, _P_KEY_K, _P_KEY_A, _P_BONUS, _P_GN_G, _P_GN_B, _P_MU_WA) = range(11)
_P_ROWS = 16


def _lane(shape):
    return lax.broadcasted_iota(jnp.int32, shape, len(shape) - 1)


def _row(shape):
    return lax.broadcasted_iota(jnp.int32, shape, len(shape) - 2)


def _nt(a, b):
    return lax.dot_general(a, b, (((1,), (1,)), ((), ())), preferred_element_type=F32)


def _tn(a, b):
    return lax.dot_general(a, b, (((0,), (0,)), ((), ())), preferred_element_type=F32)


def _nn(a, b):
    return jnp.dot(a, b, preferred_element_type=F32)


def _sigmoid(x):
    return 0.5 * jnp.tanh(0.5 * x) + 0.5


def _silu(x):
    return x * _sigmoid(x)


def _softplus(x):
    return jnp.maximum(x, 0.0) + jnp.log(1.0 + jnp.exp(-jnp.abs(x)))


def _log_sigmoid(x):
    return -_softplus(-x)


def _cumsum_rows(x, seg):
    pos = _row(x.shape) & (seg - 1)
    s = 1
    while s < seg:
        x = x + jnp.where(pos >= s, pltpu.roll(x, shift=s, axis=0), 0.0)
        s *= 2
    return x


def _cummax_rows(x, seg):
    pos = _row(x.shape) & (seg - 1)
    s = 1
    while s < seg:
        x = jnp.maximum(x, jnp.where(pos >= s, pltpu.roll(x, shift=s, axis=0), -jnp.inf))
        s *= 2
    return x


def _stack_heads(x):
    hid = _lane(x.shape) >> 6
    zero = jnp.zeros_like(x)
    return jnp.concatenate([jnp.where(hid == h, x, zero) for h in range(N_HEADS)], axis=0)


def _per_head(cols, shape):
    hid = _lane(shape) >> 6
    out = jnp.broadcast_to(cols[N_HEADS - 1], shape)
    for h in range(N_HEADS - 2, -1, -1):
        out = jnp.where(hid == h, cols[h], out)
    return out


def _head_cols(x, first):
    return [x[:, first + h:first + h + 1] for h in range(N_HEADS)]


def _block_ones(n):
    return jnp.where((_row((n, n)) >> 6) == (_lane((n, n)) >> 6), 1.0, 0.0).astype(BF16)


def _group_sum(x, ones_bd):
    hi = x.astype(BF16)
    lo = (x - hi.astype(F32)).astype(BF16)
    return _nn(hi, ones_bd) + _nn(lo, ones_bd)


def _rms_rows(x, g):
    return x * lax.rsqrt(jnp.mean(x * x, axis=-1, keepdims=True) + NORM_EPS) * g


def _params(**kw):
    return pltpu.CompilerParams(vmem_limit_bytes=VMEM_LIMIT, **kw)


def _layer_spec(shape):
    zeros = (0,) * len(shape)
    return pl.BlockSpec((None,) + tuple(shape), lambda i, layer: (layer[0],) + zeros)


def _row_spec(shape):
    zeros = (0,) * (len(shape) - 1)
    return pl.BlockSpec(tuple(shape), lambda i, layer: (i,) + zeros)


def _layered_call(body, grid, in_specs, out_specs, out_shape, scratch_shapes=(), name=None):
    def with_layer(layer_ref, *refs):
        del layer_ref
        body(*refs)

    return pl.pallas_call(
        with_layer,
        grid_spec=pltpu.PrefetchScalarGridSpec(num_scalar_prefetch=1, grid=grid, in_specs=in_specs,
                                               out_specs=out_specs, scratch_shapes=scratch_shapes),
        out_shape=out_shape, compiler_params=_params(), name=name)


def _in_proj_kernel(x_ref, g_ref, w_ref, b_ref, oa_ref, ob_ref, oc_ref, od_ref, og_ref):
    h = _rms_rows(x_ref[...], g_ref[_G_PRE:_G_PRE + 1, :]).astype(BF16)
    off = 0
    for o_ref, width in ((oa_ref, SEG_A), (ob_ref, SEG_B), (oc_ref, SEG_C), (od_ref, SEG_D), (og_ref, SEG_G)):
        acc = _nt(h, w_ref[off:off + width, :]) + b_ref[:, off:off + width]
        o_ref[...] = acc.astype(o_ref.dtype)
        off += width


def _in_proj(x2, gains, w, b, layer):
    m, d = x2.shape
    tm = min(ROW_TILE, m)
    widths = (SEG_A, SEG_B, SEG_C, SEG_D, SEG_G)
    dtypes = (BF16, BF16, BF16, BF16, F32)
    return _layered_call(
        _in_proj_kernel,
        grid=(m // tm,),
        in_specs=[_row_spec((tm, d)), _layer_spec((_G_ROWS, d)), _layer_spec((N_PROJ, d)), _layer_spec((1, N_PROJ))],
        out_specs=[_row_spec((tm, wd)) for wd in widths],
        out_shape=[jax.ShapeDtypeStruct((m, wd), dt) for wd, dt in zip(widths, dtypes)],
        name="in_proj",
    )(layer, x2, gains, w, b)


def _mlstm_kernel(a_ref, g_ref, conv_ref, ng_ref, o_ref,
                  q_s, k_s, bc_s, gt_s, bt_s, cl_s, nl_s, ml_s):
    s_len = a_ref.shape[0]
    L = MLSTM_CHUNK
    nc = s_len // L
    grp = MLSTM_GROUP
    gw = GROUP_WIDTH

    rows_of = lambda c: pl.ds(pl.multiple_of(c * L, L), L)
    taps = CONV_WIDTH_A - 1
    shift_row = _row((taps * L, 2 * L))
    shift_mats = jnp.where(_lane((taps * L, 2 * L)) == L + (shift_row & (L - 1)) - 1 - shift_row // L, 1.0, 0.0).astype(BF16)

    def conv_and_gates(c, carry):
        r0 = pl.multiple_of(c * L, L)
        cur = a_ref[rows_of(c), :2 * gw]
        before = a_ref[pl.ds(pl.multiple_of(jnp.maximum(r0 - L, 0), L), L), :2 * gw]
        before = before * jnp.where(r0 > 0, 1.0, 0.0).astype(BF16)
        shifted = _nn(shift_mats, jnp.concatenate([before, cur], axis=0))
        acc = cur.astype(F32) * conv_ref[CONV_WIDTH_A - 1:CONV_WIDTH_A, :]
        for j in range(1, CONV_WIDTH_A):
            acc = acc + shifted[(j - 1) * L:j * L, :] * conv_ref[CONV_WIDTH_A - 1 - j:CONV_WIDTH_A - j, :]
        acc = _silu(acc)
        q_s[rows_of(c), :] = acc[:, :gw].astype(BF16)
        k_s[rows_of(c), :] = (acc[:, gw:] * (HEAD_DIM ** -0.5)).astype(BF16)
        gates = g_ref[rows_of(c), :]
        bcum = _cumsum_rows(_log_sigmoid(gates), L)
        bc_s[rows_of(c), :] = bcum
        gt_s[c] = jnp.transpose(gates)
        bt_s[c] = jnp.transpose(bcum)
        return carry

    ones_bd = _block_ones(gw)
    ones_rows = jnp.where((_row((N_HEADS * L, gw)) // L) == (_lane((N_HEADS * L, gw)) >> 6), 1.0, 0.0).astype(BF16)
    tril = _lane((L, L)) <= _row((L, L))
    bd_mask = (_row((gw, gw)) >> 6) == (_lane((gw, gw)) >> 6)
    rows8 = lambda u: jnp.broadcast_to(u, (SUBLANES, u.shape[-1]))

    def local_state(i, carry):
        cs = [i * grp + j for j in range(grp)]
        for c in cs:
            conv_and_gates(c, carry)
        ks = [k_s[rows_of(c), :] for c in cs]
        e_fulls, ves = [], []
        for c in cs:
            v = a_ref[rows_of(c), 2 * gw:3 * gw]
            ig = pltpu.roll(g_ref[rows_of(c), :], shift=GATE_F - GATE_I, axis=1)
            w_loc = bc_s[pl.ds(pl.multiple_of(c * L, L) + L - 1, 1), :] - bc_s[rows_of(c), :] + ig
            m_loc = jnp.max(w_loc, axis=0, keepdims=True)
            ml_s[c] = rows8(m_loc)
            e_full = _per_head(_head_cols(jnp.exp(w_loc - m_loc), GATE_F), (L, gw))
            e_fulls.append(e_full)
            ves.append((v.astype(F32) * e_full).astype(BF16))
        c_locs = [_tn(ve, k) for ve, k in zip(ves, ks)]
        for c, c_loc, e_full, k in zip(cs, c_locs, e_fulls, ks):
            cl_s[c] = jnp.where(bd_mask, c_loc, 0.0)
            nl_s[c] = rows8(jnp.sum(e_full * k.astype(F32), axis=0, keepdims=True))
        return carry

    lax.fori_loop(0, nc // grp, local_state, 0)

    def scan(c, carry):
        c_prev, n_prev, m_prev = carry
        g_tot = bc_s[pl.ds(pl.multiple_of(c * L, L) + L - 1, 1), :]
        m_loc = ml_s[c][0:1, :]
        m_new = jnp.maximum(g_tot + m_prev, m_loc)
        a_row = jnp.exp(g_tot + m_prev - m_new)
        b_row = jnp.exp(m_loc - m_new)
        a_h, b_h = _head_cols(a_row, GATE_F), _head_cols(b_row, GATE_F)
        c_loc = cl_s[c]
        head_rows = lambda u, h: u[h * HEAD_DIM:(h + 1) * HEAD_DIM, :]
        c_new = jnp.concatenate([a_h[h] * head_rows(c_prev, h) + b_h[h] * head_rows(c_loc, h) for h in range(N_HEADS)], axis=0)
        n_new = _per_head(a_h, (1, gw)) * n_prev + _per_head(b_h, (1, gw)) * nl_s[c][0:1, :]
        return c_new, n_new, m_new

    def outputs(i, carry):
        cs = [i * grp + j for j in range(grp)]
        states = [carry]
        for c in cs:
            states.append(scan(c, states[-1]))
        hid = _lane((L, gw)) >> 6
        qs = [q_s[rows_of(c), :] for c in cs]
        ks = [k_s[rows_of(c), :] for c in cs]
        vs = [a_ref[rows_of(c), 2 * gw:3 * gw] for c in cs]
        logits = [[_nt(jnp.where(hid == h, q, jnp.zeros_like(q)), k) for h in range(N_HEADS)] for q, k in zip(qs, ks)]
        num_inter = [_nt(q, st[0].astype(BF16)) for q, st in zip(qs, states)]
        den_inter = [_group_sum(q.astype(F32) * st[1], ones_bd) for q, st in zip(qs, states)]
        shape = (L, gw)
        s_cat, iws, floors = [], [], []
        for j, c in enumerate(cs):
            b_col = bc_s[rows_of(c), :]
            g_row = gt_s[c]
            b_row = bt_s[c]
            ig = pltpu.roll(g_ref[rows_of(c), :], shift=GATE_F - GATE_I, axis=1)
            m_inter = b_col + states[j][2]
            m_t = jnp.maximum(m_inter, b_col + _cummax_rows(ig - b_col, L))
            iws.append(_per_head(_head_cols(jnp.exp(m_inter - m_t), GATE_F), shape))
            floors.append(_per_head(_head_cols(jnp.exp(-m_t), GATE_F), shape))
            col_arg = b_col - m_t
            s_list = []
            for h in range(N_HEADS):
                row_arg = g_row[GATE_I + h:GATE_I + h + 1, :] - b_row[GATE_F + h:GATE_F + h + 1, :]
                arg = jnp.where(tril, col_arg[:, GATE_F + h:GATE_F + h + 1] + row_arg, -jnp.inf)
                s_list.append((logits[j][h] * jnp.exp(arg)).astype(BF16))
            s_cat.append(jnp.concatenate(s_list, axis=1))
        num_intra = [_nn(s, _stack_heads(v)) for s, v in zip(s_cat, vs)]
        den_intra = [_nn(s, ones_rows) for s in s_cat]
        hhs = []
        for j in range(grp):
            num = num_intra[j] + iws[j] * num_inter[j]
            den = den_intra[j] + iws[j] * den_inter[j]
            hhs.append(num / jnp.maximum(jnp.abs(den), floors[j]))
        mss = [_group_sum(hh * hh, ones_bd) for hh in hhs]
        for c, hh, ms in zip(cs, hhs, mss):
            z = a_ref[rows_of(c), 3 * gw:4 * gw].astype(F32)
            y = hh * lax.rsqrt(ms * (1.0 / HEAD_DIM) + NORM_EPS) * ng_ref[...] * _silu(z)
            o_ref[rows_of(c), :] = y.astype(o_ref.dtype)
        return states[-1]

    lax.fori_loop(0, nc // grp, outputs,
                  (jnp.zeros((gw, gw), F32), jnp.zeros((1, gw), F32), jnp.zeros((1, LANES), F32)))


def _mlstm(a, gates, conv, ng, layer):
    b, s, _ = a.shape
    gw = GROUP_WIDTH
    L = MLSTM_CHUNK
    nc = s // L
    assert nc % MLSTM_GROUP == 0
    return _layered_call(
        _mlstm_kernel,
        grid=(b,),
        in_specs=[_row_spec((None, s, SEG_A)), _row_spec((None, s, SEG_G)),
                  _layer_spec((CONV_WIDTH_A, 2 * gw)), _layer_spec((1, gw))],
        out_specs=_row_spec((None, s, gw)),
        out_shape=jax.ShapeDtypeStruct((b, s, gw), BF16),
        scratch_shapes=[
            pltpu.VMEM((s, gw), BF16), pltpu.VMEM((s, gw), BF16),
            pltpu.VMEM((s, SEG_G), F32),
            pltpu.VMEM((nc, SEG_G, L), F32), pltpu.VMEM((nc, SEG_G, L), F32),
            pltpu.VMEM((nc, gw, gw), F32), pltpu.VMEM((nc, SUBLANES, gw), F32), pltpu.VMEM((nc, SUBLANES, LANES), F32),
        ],
        name="mlstm",
    )(layer, a, gates, conv, ng)


def _pool_kernel(b_ref, w_ref, sc_ref, o_ref):
    gw = GROUP_WIDTH
    u = b_ref[:, :gw].astype(F32)
    z = b_ref[:, gw:].astype(F32)
    head = POOL_WINDOWS[-1]
    grp = _lane((u.shape[0] + head, gw)) >> 6
    lane_grp = _lane((1, gw)) >> 6
    run, width = jnp.concatenate([jnp.zeros((head, gw), F32), u], axis=0), 1
    window = jnp.full((1, gw), float(POOL_WINDOWS[-1]), F32)
    sums = []
    for win in POOL_WINDOWS:
        while width < win:
            run = run + pltpu.roll(run, shift=width, axis=0)
            width *= 2
        sums.append(run)
    window_sum = sums[-1]
    for gi in range(len(POOL_WINDOWS) - 2, -1, -1):
        window_sum = jnp.where(grp == gi, sums[gi], window_sum)
        window = jnp.where(lane_grp == gi, float(POOL_WINDOWS[gi]), window)
    window_sum = window_sum[head:, :]
    pos = _row((head, gw)).astype(F32)
    first = window_sum[:head, :] / jnp.minimum(pos + 1.0, window) - u[:head, :]
    rest = window_sum[head:, :] * (1.0 / window) - u[head:, :]
    pooled = jnp.concatenate([first, rest], axis=0)
    mixed = _nn(pooled.astype(BF16), w_ref[...])
    o_ref[...] = (mixed * sc_ref[...] * _silu(z)).astype(o_ref.dtype)


def _pool(bx, w_bd, scale, layer):
    b, s, _ = bx.shape
    gw = GROUP_WIDTH
    return _layered_call(
        _pool_kernel,
        grid=(b,),
        in_specs=[_row_spec((None, s, SEG_B)), _layer_spec((gw, gw)), _layer_spec((1, gw))],
        out_specs=_row_spec((None, s, gw)),
        out_shape=jax.ShapeDtypeStruct((b, s, gw), BF16),
        name="pool",
    )(layer, bx, w_bd, scale)


def _cat_mm(xs, ys):
    return [_nn(x.astype(BF16), _stack_heads(y.astype(BF16))) for x, y in zip(xs, ys)]


def _unit_lower_inverse(mats):
    t = RWKV_CHUNK
    shift = RWKV_SUB.bit_length() - 1
    shape = mats[0].shape
    rb = _row(shape) >> shift
    col = _lane(shape) & (t - 1)
    cb = col >> shift
    eye = jnp.where(col == _row(shape), 1.0, 0.0)
    add = lambda xs, ys: [x + y for x, y in zip(xs, ys)]
    d = [jnp.where(rb == cb, a, 0.0) for a in mats]
    e = [jnp.where(rb == cb, 0.0, a) for a in mats]
    d2 = _cat_mm(d, d)
    d4 = _cat_mm(d2, d2)
    x = [eye + di for di in d]
    x = add(x, _cat_mm(x, d2))
    d8 = _cat_mm(d4, d4)
    x = add(x, _cat_mm(x, d4))
    dinv = add(x, _cat_mm(x, d8))
    f = _cat_mm(dinv, e)
    f2 = _cat_mm(f, f)
    w = [eye + fi for fi in f]
    w = add(w, _cat_mm(w, f2))
    return _cat_mm(w, dinv)


def _rwkv_kernel(c_ref, p_ref, w2a2_ref, o_ref,
                 v_s, bonus_s, pr_s, qk_s, ul_s, ar_s, ge_s, ub_s, rs_s, yc_s):
    n_seq, s_len = c_ref.shape[0], c_ref.shape[1]
    gw = GROUP_WIDTH
    T = RWKV_CHUNK
    grp = RWKV_GROUP
    rows = T * grp
    nc = s_len // T
    prm = lambda i: p_ref[i:i + 1, :]

    ones_bd = _block_ones(gw)
    col = _lane((T, N_HEADS * T)) & (T - 1)
    row = _row((T, N_HEADS * T))
    strict = col < row
    incl = col <= row
    bd_mask = (_row((gw, gw)) >> 6) == (_lane((gw, gw)) >> 6)
    first_row = _row((rows, gw)) == 0

    def prepare(e, i, carry):
        r0 = pl.multiple_of(i * rows, rows)
        tile = pl.ds(r0, rows)
        halo = c_ref[e, pl.ds(pl.multiple_of(jnp.maximum(r0 - BF16_ROWS, 0), BF16_ROWS), BF16_ROWS), :]
        prev = halo[BF16_ROWS - 1:BF16_ROWS, :].astype(F32) * jnp.where(r0 > 0, 1.0, 0.0)

        def shifted(lo, hi, mu):
            u = c_ref[e, tile, lo:hi].astype(F32)
            before = jnp.where(first_row[:, :hi - lo], prev[:, lo:hi], pltpu.roll(u, shift=1, axis=0))
            return u + mu * (before - u)

        wa = shifted(4 * gw, 4 * gw + 2 * LORA, p_ref[_P_MU_WA:_P_MU_WA + 1, :2 * LORA])
        wa = jnp.where(_lane(wa.shape) < LORA, jnp.tanh(wa), wa)
        lora = _nn(wa.astype(BF16), w2a2_ref[...])
        ld = -DECAY_SCALE * _sigmoid(prm(_P_W0) + lora[:, :gw])
        a = _sigmoid(prm(_P_A0) + lora[:, gw:])
        r = shifted(0, gw, prm(_P_MU_R))
        k = shifted(gw, 2 * gw, prm(_P_MU_K))
        v = shifted(2 * gw, 3 * gw, prm(_P_MU_V))
        kk = k * prm(_P_KEY_K)
        kk = kk * jnp.minimum(lax.rsqrt(_group_sum(kk * kk, ones_bd)), 1e12)
        k = k * (1.0 + (a - 1.0) * prm(_P_KEY_A))
        bonus_s[e, tile, :] = (_group_sum(r * k * prm(_P_BONUS), ones_bd) * v).astype(BF16)
        vb = v.astype(BF16)
        v_s[e, tile, :] = vb

        lc = _cumsum_rows(ld, T)
        g_inc = jnp.exp(lc)
        g_inv = jnp.exp(-lc)
        pt = (-kk * jnp.exp(lc - ld)).astype(BF16)
        rt = (r * g_inc).astype(BF16)
        qt = (kk * a * g_inv).astype(BF16)
        kt = (k * g_inv).astype(BF16)

        part = lambda u, j: u[j * T:(j + 1) * T, :]
        chunks = range(grp)
        prs = [jnp.concatenate([part(pt, j), part(rt, j)], axis=0) for j in chunks]
        gs = [_nt(prs[j], jnp.concatenate([_stack_heads(part(qt, j)), _stack_heads(part(kt, j))], axis=0))
              for j in chunks]
        a_pq = [jnp.where(strict, g[:T, :N_HEADS * T], 0.0) for g in gs]
        a_pk = [jnp.where(strict, g[:T, N_HEADS * T:], 0.0).astype(BF16) for g in gs]
        apv = [_nn(a_pk[j], _stack_heads(part(vb, j))) for j in chunks]
        tinv = [t.astype(BF16) for t in _unit_lower_inverse(a_pq)]
        p_fold = [_nn(tinv[j], _stack_heads(part(pt, j))) for j in chunks]
        u_loc = [_nn(tinv[j], _stack_heads(apv[j].astype(BF16))) for j in chunks]
        for j in chunks:
            c = i * grp + j
            pr_s[e, c] = jnp.concatenate([p_fold[j].astype(BF16), part(rt, j)], axis=0)
            qk_s[e, c] = jnp.concatenate([part(qt, j), part(kt, j)], axis=0)
            ul_s[e, c] = u_loc[j].astype(BF16)
            ar_s[e, c] = jnp.concatenate([jnp.where(incl, gs[j][T:, :N_HEADS * T], 0.0),
                                          jnp.where(incl, gs[j][T:, N_HEADS * T:], 0.0)], axis=1).astype(BF16)
            ge_s[e, c] = jnp.broadcast_to(part(g_inc, j)[T - 1:T, :], (SUBLANES, gw))
        return carry

    for e in range(n_seq):
        lax.fori_loop(0, nc // grp, functools.partial(prepare, e), 0)

    seqs = range(n_seq)

    rows_of = lambda c: pl.ds(pl.multiple_of(c * T, T), T)

    def project(c, states):
        return [_nt(pr_s[e, c], states[e].astype(BF16)) for e in seqs]

    def correct(c, prs):
        ubs = [(prs[e][:T] + ul_s[e, c].astype(F32)).astype(BF16) for e in seqs]
        dss = [_tn(jnp.concatenate([ubs[e], v_s[e, rows_of(c), :]], axis=0), qk_s[e, c]) for e in seqs]
        for e in seqs:
            ub_s[e, c] = ubs[e]
            rs_s[e, c] = prs[e][T:].astype(BF16)
        return dss

    def advance(c, states, dss):
        return tuple((states[e] + jnp.where(bd_mask, dss[e], 0.0)) * ge_s[e, c][0:1, :] for e in seqs)

    def readout(c):
        return [rs_s[e, c].astype(F32)
                + _nn(ar_s[e, c], jnp.concatenate([_stack_heads(ub_s[e, c]), _stack_heads(v_s[e, rows_of(c), :])], axis=0))
                for e in seqs]

    def centre(ys):
        means = [_group_sum(y, ones_bd) * (1.0 / HEAD_DIM) for y in ys]
        return [y - m for y, m in zip(ys, means)]

    def variance(ycs):
        return [_group_sum(yc * yc, ones_bd) * (1.0 / HEAD_DIM) for yc in ycs]

    def emit(c, ycs, vrs):
        for e in seqs:
            y = ycs[e] * lax.rsqrt(vrs[e] + GN_EPS) * prm(_P_GN_G) + prm(_P_GN_B) + bonus_s[e, rows_of(c), :].astype(F32)
            z = c_ref[e, rows_of(c), 3 * gw:4 * gw].astype(F32)
            o_ref[e, rows_of(c), :] = (y * _silu(z)).astype(o_ref.dtype)

    def step(c, states):
        prs = project(c, states)
        ys = readout(c - 1)
        older = [yc_s[e, c & 1] for e in seqs]
        vrs = variance(older)
        dss = correct(c, prs)
        ycs = centre(ys)
        new_states = advance(c, states, dss)
        for e in seqs:
            yc_s[e, (c - 1) & 1] = ycs[e]
        emit(c - 2, older, vrs)
        return new_states

    zero_states = tuple(jnp.zeros((gw, gw), F32) for _ in seqs)
    states = advance(0, zero_states, correct(0, project(0, zero_states)))
    first = centre(readout(0))
    states = advance(1, states, correct(1, project(1, states)))
    for e in seqs:
        yc_s[e, 0] = first[e]
    states = lax.fori_loop(2, nc, step, states, unroll=RWKV_STEP_UNROLL)
    last = centre(readout(nc - 1))
    older = [yc_s[e, (nc - 2) & 1] for e in seqs]
    emit(nc - 2, older, variance(older))
    emit(nc - 1, last, variance(last))


def _rwkv(cx, prm, w2a2, layer):
    b, s, _ = cx.shape
    gw = GROUP_WIDTH
    T = RWKV_CHUNK
    nc = s // T
    n_seq = RWKV_PAIR if b % RWKV_PAIR == 0 else 1
    assert nc % RWKV_GROUP == 0
    per_chunk = lambda r, w, dt: pltpu.VMEM((n_seq, nc, r, w), dt)
    return _layered_call(
        _rwkv_kernel,
        grid=(b // n_seq,),
        in_specs=[_row_spec((n_seq, s, SEG_C)), _layer_spec((_P_ROWS, gw)), _layer_spec((2 * LORA, 2 * gw))],
        out_specs=_row_spec((n_seq, s, gw)),
        out_shape=jax.ShapeDtypeStruct((b, s, gw), BF16),
        scratch_shapes=[
            pltpu.VMEM((n_seq, s, gw), BF16), pltpu.VMEM((n_seq, s, gw), BF16),
            per_chunk(2 * T, gw, BF16), per_chunk(2 * T, gw, BF16),
            per_chunk(T, gw, BF16),
            per_chunk(T, 2 * N_HEADS * T, BF16), per_chunk(SUBLANES, gw, F32),
            per_chunk(T, gw, BF16), per_chunk(T, gw, BF16),
            pltpu.VMEM((n_seq, 2, T, gw), F32),
        ],
        name="rwkv",
    )(layer, cx, prm, w2a2)


def _fox_kernel(d_ref, g_ref, qkg_ref, o_ref, q_s, k_s, vt_s, ft_s, l_s):
    s_len = d_ref.shape[0]
    gw = GROUP_WIDTH
    tq = min(FOX_QBLOCK, s_len)
    ones_bd = _block_ones(gw)

    def head_norm(u, g):
        ms = _group_sum(u * u, ones_bd) * (1.0 / HEAD_DIM)
        return u * lax.rsqrt(ms + NORM_EPS) * g

    q_s[...] = (head_norm(d_ref[:, 0:gw].astype(F32), qkg_ref[0:1, :]) * (LOG2_E * HEAD_DIM ** -0.5)).astype(BF16)
    k_s[...] = head_norm(d_ref[:, gw:2 * gw].astype(F32), qkg_ref[1:2, :]).astype(BF16)
    v_t = jnp.transpose(d_ref[:, 2 * gw:3 * gw].astype(F32)).astype(BF16)
    ones_rows = jnp.where(_row((BF16_ROWS, s_len)) == 0, 1.0, 0.0).astype(BF16)
    for h in range(N_HEADS):
        vt_s[h, 0:HEAD_DIM, :] = v_t[h * HEAD_DIM:(h + 1) * HEAD_DIM, :]
        vt_s[h, HEAD_DIM:, :] = ones_rows
    ft_s[...] = jnp.transpose(_cumsum_rows(_log_sigmoid(g_ref[...]), s_len) * LOG2_E)

    heads = range(N_HEADS)
    tril = _lane((tq, tq)) <= _row((tq, tq))
    for blk in range(s_len // tq):
        q0 = blk * tq
        q_st = _stack_heads(q_s[q0:q0 + tq, :])
        mx = [None] * N_HEADS
        for j in range(blk + 1):
            k0 = j * tq
            scores = _nt(q_st, k_s[k0:k0 + tq, :])
            for h in heads:
                lg = scores[h * tq:(h + 1) * tq, :] - ft_s[GATE_D + h:GATE_D + h + 1, k0:k0 + tq]
                if j == blk:
                    lg = jnp.where(tril, lg, -jnp.inf)
                l_s[h, :, k0:k0 + tq] = lg
                tile_max = jnp.max(lg, axis=-1, keepdims=True)
                mx[h] = tile_max if j == 0 else jnp.maximum(mx[h], tile_max)
        outs = []
        for h in heads:
            acc = None
            for j in range(blk + 1):
                k0 = j * tq
                p = jnp.exp2(l_s[h, :, k0:k0 + tq] - mx[h]).astype(BF16)
                part = _nt(vt_s[h, :, k0:k0 + tq], p)
                acc = part if acc is None else acc + part
            outs.append(acc[0:HEAD_DIM, :] * (1.0 / acc[HEAD_DIM:HEAD_DIM + 1, :]))
        out = jnp.transpose(jnp.concatenate(outs, axis=0))
        z = d_ref[q0:q0 + tq, 3 * gw:4 * gw].astype(F32)
        o_ref[q0:q0 + tq, :] = (out * _silu(z)).astype(o_ref.dtype)


def _fox(dx, gates, qkg, layer):
    b, s, _ = dx.shape
    gw = GROUP_WIDTH
    return _layered_call(
        _fox_kernel,
        grid=(b,),
        in_specs=[_row_spec((None, s, SEG_D)), _row_spec((None, s, SEG_G)), _layer_spec((2, gw))],
        out_specs=_row_spec((None, s, gw)),
        out_shape=jax.ShapeDtypeStruct((b, s, gw), BF16),
        scratch_shapes=[pltpu.VMEM((s, gw), BF16), pltpu.VMEM((s, gw), BF16),
                        pltpu.VMEM((N_HEADS, HEAD_DIM + BF16_ROWS, s), BF16), pltpu.VMEM((SEG_G, s), F32),
                        pltpu.VMEM((N_HEADS, min(FOX_QBLOCK, s), s), F32)],
        name="fox",
    )(layer, dx, gates, qkg)


def _mem_kv_kernel(mem_ref, g_ref, w_ref, o_ref):
    b, n_mem, d = mem_ref.shape
    mn = _rms_rows(mem_ref[...].reshape(b * n_mem, d), g_ref[...]).astype(BF16)
    o_ref[...] = _nn(mn, w_ref[...]).astype(o_ref.dtype).reshape(o_ref.shape)


def _mem_kv(mem, g, wkv):
    b, n_mem, d = mem.shape
    depth = wkv.shape[0]
    return pl.pallas_call(
        _mem_kv_kernel,
        grid=(depth,),
        in_specs=[
            pl.BlockSpec((b, n_mem, d), lambda l: (0, 0, 0)),
            pl.BlockSpec((1, d), lambda l: (0, 0)),
            pl.BlockSpec((None, d, 2 * d), lambda l: (l, 0, 0)),
        ],
        out_specs=pl.BlockSpec((None, b, n_mem, 2 * d), lambda l: (l, 0, 0, 0)),
        out_shape=jax.ShapeDtypeStruct((depth, b, n_mem, 2 * d), BF16),
        compiler_params=_params(),
        name="mem_kv",
    )(mem, g, wkv)


def _out_xattn_kernel(x_ref, ya_ref, yb_ref, yc_ref, yd_ref, wout_ref, g_ref, wq_ref, kv_ref, wo_ref,
                      o_ref, *, n_mem_heads):
    d = x_ref.shape[-1]
    dh = d // n_mem_heads
    tm = x_ref.shape[0]
    gain = lambda i: g_ref[i:i + 1, :]
    parts = [pl.ds(j * (tm // XATTN_PARTS), tm // XATTN_PARTS) for j in range(XATTN_PARTS)]
    ys = [jnp.concatenate([ya_ref[r, :], yb_ref[r, :], yc_ref[r, :], yd_ref[r, :]], axis=-1) for r in parts]
    mixed = [_nn(y, wout_ref[...]) for y in ys]
    xs = [x_ref[r, :] + _rms_rows(m, gain(_G_POST)) for r, m in zip(parts, mixed)]
    hms = [_rms_rows(x, gain(_G_XPRE)).astype(BF16) for x in xs]
    qs = [_nn(hm, wq_ref[...]).astype(BF16) for hm in hms]
    outs = [[] for _ in parts]
    for h in range(n_mem_heads):
        kh = kv_ref[:, h * dh:(h + 1) * dh]
        vh = kv_ref[:, d + h * dh:d + (h + 1) * dh]
        logits = [_nt(q[:, h * dh:(h + 1) * dh], kh) * (dh ** -0.5) for q in qs]
        ps = [jnp.exp(l - jnp.max(l, axis=-1, keepdims=True)) for l in logits]
        ps = [(p / jnp.sum(p, axis=-1, keepdims=True)).astype(BF16) for p in ps]
        for j, p in enumerate(ps):
            outs[j].append(_nn(p, vh).astype(BF16))
    attn = [_nn(jnp.concatenate(o, axis=-1), wo_ref[...]) for o in outs]
    for r, x, o in zip(parts, xs, attn):
        o_ref[r, :] = x + _rms_rows(o, gain(_G_XPOST))


def _out_xattn(x2, ys, wout, gains, wq, kv, wo, seq_len, layer, n_mem_heads=4):
    m, d = x2.shape
    gw = GROUP_WIDTH
    tm = min(XATTN_ROW_TILE, seq_len)
    per_seq = seq_len // tm
    n_mem = kv.shape[2]
    return _layered_call(
        functools.partial(_out_xattn_kernel, n_mem_heads=n_mem_heads),
        grid=(m // tm,),
        in_specs=[_row_spec((tm, d))] + [_row_spec((tm, gw))] * 4 + [
            _layer_spec((4 * gw, d)),
            _layer_spec((_G_ROWS, d)),
            _layer_spec((d, d)),
            pl.BlockSpec((None, None, n_mem, 2 * d), lambda i, layer: (layer[0], i // per_seq, 0, 0)),
            _layer_spec((d, d)),
        ],
        out_specs=_row_spec((tm, d)),
        out_shape=jax.ShapeDtypeStruct((m, d), F32),
        name="out_xattn",
    )(layer, x2, *ys, wout, gains, wq, kv, wo)


_IN_SIZES = (("a_q", GROUP_WIDTH), ("a_k", GROUP_WIDTH), ("a_v", GROUP_WIDTH), ("a_i", N_HEADS), ("a_f", N_HEADS),
             ("a_z", GROUP_WIDTH), ("b_x", GROUP_WIDTH), ("b_z", GROUP_WIDTH),
             ("c_r", GROUP_WIDTH), ("c_k", GROUP_WIDTH), ("c_v", GROUP_WIDTH), ("c_w", LORA), ("c_a", LORA),
             ("c_z", GROUP_WIDTH), ("d_q", GROUP_WIDTH), ("d_k", GROUP_WIDTH), ("d_v", GROUP_WIDTH),
             ("d_f", N_HEADS), ("d_z", GROUP_WIDTH))
_IN_ORDER = ("a_q", "a_k", "a_v", "a_z", "b_x", "b_z", "c_r", "c_k", "c_v", "c_z", "c_w", "c_a",
             "d_q", "d_k", "d_v", "d_z", "a_i", "a_f", "d_f")
N_IN = sum(size for _, size in _IN_SIZES)


def _relayout_in(w, axis):
    pieces, off = {}, 0
    for name, size in _IN_SIZES:
        pieces[name] = lax.slice_in_dim(w, off, off + size, axis=axis)
        off += size
    pad_shape = list(w.shape)
    pad_shape[axis] = SEG_G - 3 * N_HEADS
    return jnp.concatenate([pieces[n] for n in _IN_ORDER] + [jnp.zeros(pad_shape, w.dtype)], axis=axis)


def _relayout_w_kernel(w_ref, o_ref):
    src = {}
    off = 0
    for name, size in _IN_SIZES:
        src[name] = (off, size)
        off += size
    dst = 0
    gate_rows = []
    for name in _IN_ORDER:
        start, size = src[name]
        if size < BF16_ROWS:
            gate_rows.append(w_ref[start:start + size, :])
        else:
            o_ref[dst:dst + size, :] = w_ref[start:start + size, :].astype(o_ref.dtype)
            dst += size
    used = sum(g.shape[0] for g in gate_rows)
    gate_rows.append(jnp.zeros((o_ref.shape[0] - dst - used, o_ref.shape[1]), F32))
    o_ref[dst:, :] = jnp.concatenate(gate_rows, axis=0).astype(o_ref.dtype)


def _relayout_w(w_t):
    depth, n_in, d = w_t.shape
    assert n_in == N_IN
    tc = RELAYOUT_COLS
    return pl.pallas_call(
        _relayout_w_kernel,
        grid=(depth, d // tc),
        in_specs=[pl.BlockSpec((None, n_in, tc), lambda l, i: (l, 0, i))],
        out_specs=pl.BlockSpec((None, N_PROJ, tc), lambda l, i: (l, 0, i)),
        out_shape=jax.ShapeDtypeStruct((depth, N_PROJ, d), BF16),
        compiler_params=_params(),
        name="relayout_w",
    )(w_t)


def _block_diag(blocks):
    n, k, _ = blocks.shape
    eye = jnp.eye(n, dtype=blocks.dtype)
    return jnp.einsum("gcd,gh->gchd", blocks, eye).reshape(n * k, n * k)


def kernel(x, mem, pre_norm_g, post_norm_g, w_in, b_in, conv_a, norm_a_g, pool_w, pool_scale, shift_mu_c,
           decay_w0, decay_w2, iclr_a0, iclr_a2, key_k, key_a, bonus_u, gn_c_g, gn_c_b, qk_norm_d, w_out,
           mem_norm_g, xattn_pre_g, xattn_post_g, xattn_wq, xattn_wkv, xattn_wo):
    bsz, seq, d = x.shape
    depth = w_in.shape[0]
    gw = GROUP_WIDTH

    w_in_r = _relayout_w(jnp.swapaxes(w_in, 1, 2))
    b_in_r = _relayout_in(b_in, 1).reshape(depth, 1, N_PROJ)
    gains = jnp.stack([pre_norm_g, post_norm_g, xattn_pre_g, xattn_post_g]
                      + [jnp.zeros_like(pre_norm_g)] * (_G_ROWS - 4), axis=1)
    mu = shift_mu_c
    zeros_row = jnp.zeros((depth, gw), F32)
    mu_wa = jnp.concatenate([mu[:, 3 * gw:], jnp.zeros((depth, gw - 2 * LORA), F32)], axis=-1)
    rows = [mu[:, 0:gw], mu[:, gw:2 * gw], mu[:, 2 * gw:3 * gw], decay_w0, iclr_a0, key_k, key_a, bonus_u,
            gn_c_g, gn_c_b, mu_wa] + [zeros_row] * (_P_ROWS - 11)
    rwkv_prm = jnp.stack(rows, axis=1)
    zl = jnp.zeros((depth, LORA, gw), F32)
    w2a2 = jnp.concatenate([jnp.concatenate([decay_w2, zl], axis=-1),
                            jnp.concatenate([zl, iclr_a2], axis=-1)], axis=1).astype(BF16)
    pool_bd = jax.vmap(_block_diag)(pool_w).astype(BF16)
    qkg = jnp.tile(qk_norm_d, (1, 1, N_HEADS))
    norm_a = norm_a_g.reshape(depth, 1, gw)
    pool_sc = pool_scale.reshape(depth, 1, gw)
    w_out_b = w_out.astype(BF16)
    wq_b = xattn_wq.astype(BF16)
    wo_b = xattn_wo.astype(BF16)

    kv_all = _mem_kv(mem, mem_norm_g.reshape(1, d), xattn_wkv.astype(BF16))

    x2 = x.reshape(bsz * seq, d)
    sh = lambda u: u.reshape(bsz, seq, u.shape[-1])
    for l in range(depth):
        layer = jnp.full((1,), l, jnp.int32)
        pa, pb, pc, pd, pg = _in_proj(x2, gains, w_in_r, b_in_r, layer)
        gates = sh(pg)
        y_a = _mlstm(sh(pa), gates, conv_a, norm_a, layer)
        y_b = _pool(sh(pb), pool_bd, pool_sc, layer)
        y_c = _rwkv(sh(pc), rwkv_prm, w2a2, layer)
        y_d = _fox(sh(pd), gates, qkg, layer)
        ys = [u.reshape(bsz * seq, gw) for u in (y_a, y_b, y_c, y_d)]
        x2 = _out_xattn(x2, ys, w_out_b, gains, wq_b, kv_all, wo_b, seq, layer)
    return x2.reshape(bsz, seq, d)
```
